```python
import math
import jax, jax.numpy as jnp
from jax import lax
import numpy as np

D_MODEL = 1024
BATCH = 4
SEQ = 8192
DEPTH = 4

GRID_W = 64
CTX_LEN = 256
N_MIXERS = 3
ALPHA = (2 * DEPTH) ** 0.25
BETA = (8 * DEPTH) ** -0.25
LN_EPS = 1e-5
SCAN_BLOCK = 128
S5_GROUP = 16
S5_GROUPS = D_MODEL // S5_GROUP
S5_STATE = 64
ML_INNER = 2 * D_MODEL
ML_HEADS = 4
ML_HEAD_DIM = ML_INNER // ML_HEADS
ML_CONV = 3
CV_KERNEL = 31
FFN_HIDDEN = ((8 * D_MODEL // 3 + 255) // 256) * 256
FFN_CONV = 3
N_S5_LAYERS = (DEPTH + 2) // 3
N_ML_LAYERS = (DEPTH + 1) // 3
N_CV_LAYERS = DEPTH // 3

kernel_name = 'hybrid_s5_mlstm_conformer_prefix_trunk'

F32 = jnp.float32


def layer_norm(x, g, b):
    xf = x.astype(F32)
    mu = xf.mean(-1, keepdims=True)
    var = jnp.square(xf - mu).mean(-1, keepdims=True)
    return ((xf - mu) * lax.rsqrt(var + LN_EPS) * g + b).astype(x.dtype)


def dwconv1d(x, w, b):
    k = w.shape[0]
    y = lax.conv_general_dilated(x, w[:, None, :].astype(x.dtype), (1,), [((k - 1) // 2, k // 2)],
                                 dimension_numbers=('NWC', 'WIO', 'NWC'), feature_group_count=x.shape[-1])
    return y + b


def dwconv2d(x, w, b):
    kh, kw = w.shape[:2]
    y = lax.conv_general_dilated(x, w[:, :, None, :].astype(x.dtype), (1, 1),
                                 [((kh - 1) // 2, kh // 2), ((kw - 1) // 2, kw // 2)],
                                 dimension_numbers=('NHWC', 'HWIO', 'NHWC'), feature_group_count=x.shape[-1])
    return y + b


def grid_transpose(x, rows, cols):
    bsz, _, ch = x.shape
    return x.reshape(bsz, rows, cols, ch).transpose(0, 2, 1, 3).reshape(bsz, rows * cols, ch)


def s5_discretise(lam_re, lam_im, log_dt, b_re, b_im):
    lre = jnp.minimum(lam_re.astype(F32), -1e-4)
    lim = lam_im.astype(F32)
    dt = jnp.exp(log_dt.astype(F32))[:, None]
    mag = jnp.exp(lre * dt)
    lb_re, lb_im = mag * jnp.cos(lim * dt), mag * jnp.sin(lim * dt)
    nr, ni = lb_re - 1.0, lb_im
    den = lre * lre + lim * lim
    cr = (nr * lre + ni * lim) / den
    ci = (ni * lre - nr * lim) / den
    br, bi = b_re.astype(F32), b_im.astype(F32)
    bb_re = cr[..., None] * br - ci[..., None] * bi
    bb_im = cr[..., None] * bi + ci[..., None] * br
    return lb_re, lb_im, bb_re, bb_im


def _cplx_affine_combine(e1, e2):
    a1r, a1i, b1r, b1i = e1
    a2r, a2i, b2r, b2i = e2
    return (a2r * a1r - a2i * a1i, a2r * a1i + a2i * a1r,
            a2r * b1r - a2i * b1i + b2r, a2r * b1i + a2i * b1r + b2i)


def s5_scan(u, lb_re, lb_im, bb_re, bb_im, c_re, c_im, s0):
    bsz, length, _ = u.shape
    nblk = length // SCAN_BLOCK
    ub = jnp.moveaxis(u.reshape(bsz, nblk, SCAN_BLOCK, S5_GROUPS, S5_GROUP), 1, 0)

    def step(carry, u_blk):
        sr, si = carry
        bu_re = jnp.einsum('btgs,gps->btgp', u_blk, bb_re)
        bu_im = jnp.einsum('btgs,gps->btgp', u_blk, bb_im)
        a_re = jnp.broadcast_to(lb_re, bu_re.shape)
        a_im = jnp.broadcast_to(lb_im, bu_im.shape)
        pa_re, pa_im, pb_re, pb_im = lax.associative_scan(_cplx_affine_combine, (a_re, a_im, bu_re, bu_im), axis=1)
        st_re = pa_re * sr[:, None] - pa_im * si[:, None] + pb_re
        st_im = pa_re * si[:, None] + pa_im * sr[:, None] + pb_im
        y = jnp.einsum('gsp,btgp->btgs', c_re, st_re) - jnp.einsum('gsp,btgp->btgs', c_im, st_im)
        return (st_re[:, -1], st_im[:, -1]), y.reshape(bsz, SCAN_BLOCK, D_MODEL)

    s_end, ys = lax.scan(step, s0, ub)
    return jnp.moveaxis(ys, 0, 1).reshape(bsz, length, D_MODEL), s_end


def s5_mixer(u_lat, u_ctx, lam_re, lam_im, log_dt, b_re, b_im, c_re, c_im, d_skip, w_glu, b_glu):
    ul, uc = u_lat.astype(F32), u_ctx.astype(F32)
    d = d_skip.astype(F32)
    y_lat, y_ctx = d * ul, d * uc
    bsz = ul.shape[0]
    for direction in range(2):
        lb_re, lb_im, bb_re, bb_im = s5_discretise(lam_re[direction], lam_im[direction], log_dt[direction],
                                                   b_re[direction], b_im[direction])
        cr, ci = c_re[direction].astype(F32), c_im[direction].astype(F32)
        zero = jnp.zeros((bsz, S5_GROUPS, S5_STATE), F32)
        rev = (lambda a: a[:, ::-1]) if direction == 1 else (lambda a: a)
        yc, s_ctx = s5_scan(rev(uc), lb_re, lb_im, bb_re, bb_im, cr, ci, (zero, zero))
        yl, _ = s5_scan(rev(ul), lb_re, lb_im, bb_re, bb_im, cr, ci, s_ctx)
        y_ctx = y_ctx + rev(yc)
        y_lat = y_lat + rev(yl)

    def glu(y):
        z = jax.nn.gelu(y).astype(u_lat.dtype) @ w_glu + b_glu
        val, gate = jnp.split(z, 2, axis=-1)
        return val * jax.nn.sigmoid(gate)

    return glu(y_lat), glu(y_ctx)


def mlstm_scan(q, k, v, ig, lf, state):
    bsz, nh, length, dh = q.shape
    nblk = length // SCAN_BLOCK
    blocks = lambda a: jnp.moveaxis(a.reshape(bsz, nh, nblk, SCAN_BLOCK, *a.shape[3:]), 2, 0)
    tri = jnp.tril(jnp.ones((SCAN_BLOCK, SCAN_BLOCK), bool))

    def step(carry, blk):
        C, n, m = carry
        qb, kb, vb, ib, fb = blk
        b = jnp.cumsum(fb, axis=-1)
        logd = jnp.where(tri, b[..., :, None] - b[..., None, :] + ib[..., None, :], -jnp.inf)
        m_inter = b + m[..., None]
        m_t = jnp.maximum(m_inter, logd.max(-1))
        w_inter = jnp.exp(m_inter - m_t)
        s = jnp.exp(logd - m_t[..., None]) * jnp.einsum('bhtd,bhsd->bhts', qb, kb)
        num = w_inter[..., None] * jnp.einsum('bhed,bhtd->bhte', C, qb) + jnp.einsum('bhts,bhse->bhte', s, vb)
        den = w_inter * jnp.einsum('bhd,bhtd->bht', n, qb) + s.sum(-1)
        h = num / jnp.maximum(jnp.abs(den), jnp.exp(-m_t))[..., None]
        decay = b[..., -1:] - b + ib
        m_new = jnp.maximum(b[..., -1] + m, decay.max(-1))
        w_prev = jnp.exp(b[..., -1] + m - m_new)
        w_r = jnp.exp(decay - m_new[..., None])
        C = w_prev[..., None, None] * C + jnp.einsum('bhse,bhsd->bhed', w_r[..., None] * vb, kb)
        n = w_prev[..., None] * n + jnp.einsum('bhs,bhsd->bhd', w_r, kb)
        return (C, n, m_new), h

    state, hs = lax.scan(step, state, tuple(blocks(a) for a in (q, k, v, ig, lf)))
    return jnp.moveaxis(hs, 0, 2).reshape(bsz, nh, length, dh), state


def mlstm_project(u, w_up, conv_w, conv_b, w_q, w_k, w_v, w_gates, b_gates):
    bsz, length, _ = u.shape
    xm, z = jnp.split(u @ w_up, 2, axis=-1)
    xc = jax.nn.silu(dwconv1d(xm, conv_w, conv_b))
    q = xc @ w_q
    k = (xc @ w_k) * ML_HEAD_DIM ** -0.5
    v = xm @ w_v
    g = q @ w_gates[0] + k @ w_gates[1] + v @ w_gates[2] + b_gates
    heads = lambda a: a.reshape(bsz, length, ML_HEADS, ML_HEAD_DIM).transpose(0, 2, 1, 3).astype(F32)
    return heads(q), heads(k), heads(v), g.reshape(bsz, length, 2, 2, ML_HEADS).astype(F32), xc, z


def mlstm_out(h, xc, z, gn_g, skip, w_down):
    bsz, nh, length, dh = h.shape
    mu = h.mean(-1, keepdims=True)
    var = jnp.square(h - mu).mean(-1, keepdims=True)
    hn = ((h - mu) * lax.rsqrt(var + LN_EPS)).transpose(0, 2, 1, 3).reshape(bsz, length, nh * dh)
    hn = (hn * gn_g).astype(xc.dtype) + skip * xc
    return (hn * jax.nn.silu(z)) @ w_down


def mlstm_mixer(u_lat, u_ctx, w_up, conv_w, conv_b, w_q, w_k, w_v, w_gates, b_gates, gn_g, skip, w_down):
    lat = mlstm_project(u_lat, w_up, conv_w, conv_b, w_q, w_k, w_v, w_gates, b_gates)
    cxt = mlstm_project(u_ctx, w_up, conv_w, conv_b, w_q, w_k, w_v, w_gates, b_gates)
    bsz = u_lat.shape[0]
    h_lat = jnp.zeros(lat[0].shape, F32)
    h_ctx = jnp.zeros(cxt[0].shape, F32)
    for direction in range(2):
        rev = (lambda a: jnp.flip(a, axis=2)) if direction == 1 else (lambda a: a)

        def prep(p):
            q, k, v, g = p[:4]
            ig = jnp.moveaxis(g[:, :, direction, 0], 1, 2)
            lf = jax.nn.log_sigmoid(jnp.moveaxis(g[:, :, direction, 1], 1, 2))
            return tuple(rev(a) for a in (q, k, v, ig, lf))

        st0 = (jnp.zeros((bsz, ML_HEADS, ML_HEAD_DIM, ML_HEAD_DIM), F32),
               jnp.zeros((bsz, ML_HEADS, ML_HEAD_DIM), F32),
               jnp.zeros((bsz, ML_HEADS), F32))
        hc, st_ctx = mlstm_scan(*prep(cxt), st0)
        hl, _ = mlstm_scan(*prep(lat), st_ctx)
        h_ctx = h_ctx + rev(hc)
        h_lat = h_lat + rev(hl)
    return (mlstm_out(h_lat, lat[4], lat[5], gn_g, skip, w_down),
            mlstm_out(h_ctx, cxt[4], cxt[5], gn_g, skip, w_down))


def conformer_conv(u, w_in, b_in, dw_w, dw_b, ln_g, ln_b, w_out, b_out):
    a, gate = jnp.split(u @ w_in + b_in, 2, axis=-1)
    hmid = a * jax.nn.sigmoid(gate)
    hmid = jax.nn.silu(layer_norm(dwconv1d(hmid, dw_w, dw_b), ln_g, ln_b))
    return hmid @ w_out + b_out


def conv_ffn(u, w_gate, w_up, conv_w, conv_b, w_down, rows, cols):
    bsz, length, _ = u.shape
    gate = dwconv2d((u @ w_gate).reshape(bsz, rows, cols, -1), conv_w, conv_b).reshape(bsz, length, -1)
    return (jax.nn.gelu(gate) * (u @ w_up)) @ w_down


def setup_inputs(seed: int = 0) -> dict:
    key = jax.random.key(seed)
    ks = iter(jax.random.split(key, 64))
    nrm = lambda shape, scale: scale * jax.random.normal(next(ks), shape, F32)
    D, F, E, G, P, GS, H = D_MODEL, FFN_HIDDEN, ML_INNER, S5_GROUPS, S5_STATE, S5_GROUP, ML_HEADS
    NA, NB, NC = N_S5_LAYERS, N_ML_LAYERS, N_CV_LAYERS
    inp = {}
    inp['x'] = nrm((BATCH, SEQ, D), 1.0)
    inp['c'] = nrm((BATCH, D), 1.0)
    inp['ctx'] = nrm((BATCH, CTX_LEN, D), 1.0)
    inp['c_ctx'] = nrm((D,), 1.0)
    inp['mod_w'] = nrm((DEPTH, D, 6 * D), 0.2 * D ** -0.5)
    inp['mod_b'] = nrm((DEPTH, 6 * D), 0.02)
    inp['post_ln_g'] = 1.0 + nrm((DEPTH, 2, D), 0.02)
    inp['post_ln_b'] = nrm((DEPTH, 2, D), 0.02)
    inp['ffn_w_gate'] = nrm((DEPTH, D, F), D ** -0.5)
    inp['ffn_w_up'] = nrm((DEPTH, D, F), D ** -0.5)
    inp['ffn_conv_w'] = nrm((DEPTH, FFN_CONV, FFN_CONV, F), 1.0 / FFN_CONV)
    inp['ffn_conv_b'] = nrm((DEPTH, F), 0.02)
    inp['ffn_w_down'] = nrm((DEPTH, F, D), BETA * F ** -0.5)
    inp['s5_lambda_re'] = -0.5 + nrm((NA, 2, G, P), 0.01)
    inp['s5_lambda_im'] = math.pi * jnp.arange(P, dtype=F32) + nrm((NA, 2, G, P), 0.01)
    inp['s5_log_dt'] = jax.random.uniform(next(ks), (NA, 2, G), F32, math.log(1e-3), math.log(1e-1))
    inp['s5_b_re'] = nrm((NA, 2, G, P, GS), (2 * GS) ** -0.5)
    inp['s5_b_im'] = nrm((NA, 2, G, P, GS), (2 * GS) ** -0.5)
    inp['s5_c_re'] = nrm((NA, 2, G, GS, P), (2 * P) ** -0.5)
    inp['s5_c_im'] = nrm((NA, 2, G, GS, P), (2 * P) ** -0.5)
    inp['s5_d'] = nrm((NA, D), 1.0)
    inp['s5_w_glu'] = jnp.concatenate([nrm((NA, D, D), BETA * D ** -0.5), nrm((NA, D, D), D ** -0.5)], axis=-1)
    inp['s5_b_glu'] = nrm((NA, 2 * D), 0.01)
    inp['ml_w_up'] = nrm((NB, D, 2 * E), D ** -0.5)
    inp['ml_conv_w'] = nrm((NB, ML_CONV, E), ML_CONV ** -0.5)
    inp['ml_conv_b'] = nrm((NB, E), 0.02)
    inp['ml_w_q'] = nrm((NB, E, E), E ** -0.5)
    inp['ml_w_k'] = nrm((NB, E, E), E ** -0.5)
    inp['ml_w_v'] = nrm((NB, E, E), E ** -0.5)
    inp['ml_w_gates'] = nrm((NB, 3, E, 4 * H), 0.1 * (3 * E) ** -0.5)
    ig_b = nrm((NB, 2, 1, H), 0.1)
    fg_b = jnp.linspace(3.0, 6.0, H, dtype=F32) + nrm((NB, 2, 1, H), 0.1)
    inp['ml_b_gates'] = jnp.concatenate([ig_b, fg_b], axis=2).reshape(NB, 4 * H)
    inp['ml_gn_g'] = 1.0 + nrm((NB, E), 0.02)
    inp['ml_skip'] = 1.0 + nrm((NB, E), 0.02)
    inp['ml_w_down'] = nrm((NB, E, D), BETA * E ** -0.5)
    inp['cv_w_in'] = nrm((NC, D, 2 * D), D ** -0.5)
    inp['cv_b_in'] = nrm((NC, 2 * D), 0.02)
    inp['cv_dw_w'] = nrm((NC, CV_KERNEL, D), CV_KERNEL ** -0.5)
    inp['cv_dw_b'] = nrm((NC, D), 0.02)
    inp['cv_ln_g'] = 1.0 + nrm((NC, D), 0.02)
    inp['cv_ln_b'] = nrm((NC, D), 0.02)
    inp['cv_w_out'] = nrm((NC, D, D), BETA * D ** -0.5)
    inp['cv_b_out'] = nrm((NC, D), 0.02)
    return inp


def reference(x, c, ctx, c_ctx, mod_w, mod_b, post_ln_g, post_ln_b,
              ffn_w_gate, ffn_w_up, ffn_conv_w, ffn_conv_b, ffn_w_down,
              s5_lambda_re, s5_lambda_im, s5_log_dt, s5_b_re, s5_b_im, s5_c_re, s5_c_im, s5_d, s5_w_glu, s5_b_glu,
              ml_w_up, ml_conv_w, ml_conv_b, ml_w_q, ml_w_k, ml_w_v, ml_w_gates, ml_b_gates, ml_gn_g, ml_skip, ml_w_down,
              cv_w_in, cv_b_in, cv_dw_w, cv_dw_b, cv_ln_g, cv_ln_b, cv_w_out, cv_b_out):
    length = x.shape[1]
    rows = length // GRID_W
    ctx_len = ctx.shape[1]
    silu_c = jax.nn.silu(c)[:, None, :]
    silu_cc = jax.nn.silu(c_ctx)[None, None, :]
    h_lat, h_ctx = x, ctx
    for i in range(DEPTH):
        kind, occ = i % N_MIXERS, i // N_MIXERS
        last = i == DEPTH - 1
        sh1, sc1, g1, sh2, sc2, g2 = jnp.split(silu_c @ mod_w[i] + mod_b[i], 6, axis=-1)
        csh1, csc1, cg1, csh2, csc2, cg2 = jnp.split(silu_cc @ mod_w[i] + mod_b[i], 6, axis=-1)
        u_lat = h_lat * (1 + sc1) + sh1
        u_ctx = h_ctx * (1 + csc1) + csh1
        col_major = (kind != 2) and (occ % 2 == 1)
        if col_major:
            u_lat = grid_transpose(u_lat, rows, GRID_W)
        if kind == 0:
            y_lat, y_ctx = s5_mixer(u_lat, u_ctx, s5_lambda_re[occ], s5_lambda_im[occ], s5_log_dt[occ],
                                    s5_b_re[occ], s5_b_im[occ], s5_c_re[occ], s5_c_im[occ], s5_d[occ],
                                    s5_w_glu[occ], s5_b_glu[occ])
        elif kind == 1:
            y_lat, y_ctx = mlstm_mixer(u_lat, u_ctx, ml_w_up[occ], ml_conv_w[occ], ml_conv_b[occ], ml_w_q[occ],
                                       ml_w_k[occ], ml_w_v[occ], ml_w_gates[occ], ml_b_gates[occ], ml_gn_g[occ],
                                       ml_skip[occ], ml_w_down[occ])
        else:
            y_lat = conformer_conv(u_lat, cv_w_in[occ], cv_b_in[occ], cv_dw_w[occ], cv_dw_b[occ],
                                   cv_ln_g[occ], cv_ln_b[occ], cv_w_out[occ], cv_b_out[occ])
            y_ctx = conformer_conv(u_ctx, cv_w_in[occ], cv_b_in[occ], cv_dw_w[occ], cv_dw_b[occ],
                                   cv_ln_g[occ], cv_ln_b[occ], cv_w_out[occ], cv_b_out[occ])
        if col_major:
            y_lat = grid_transpose(y_lat, GRID_W, rows)
        h_lat = layer_norm(ALPHA * h_lat + (1 + g1) * y_lat, post_ln_g[i, 0], post_ln_b[i, 0])
        f_lat = conv_ffn(h_lat * (1 + sc2) + sh2, ffn_w_gate[i], ffn_w_up[i], ffn_conv_w[i], ffn_conv_b[i],
                         ffn_w_down[i], rows, GRID_W)
        h_lat = layer_norm(ALPHA * h_lat + (1 + g2) * f_lat, post_ln_g[i, 1], post_ln_b[i, 1])
        if not last:
            h_ctx = layer_norm(ALPHA * h_ctx + (1 + cg1) * y_ctx, post_ln_g[i, 0], post_ln_b[i, 0])
            f_ctx = conv_ffn(h_ctx * (1 + csc2) + csh2, ffn_w_gate[i], ffn_w_up[i], ffn_conv_w[i], ffn_conv_b[i],
                             ffn_w_down[i], 1, ctx_len)
            h_ctx = layer_norm(ALPHA * h_ctx + (1 + cg2) * f_ctx, post_ln_g[i, 1], post_ln_b[i, 1])
    return h_lat
```

```python
import functools
import math

import jax
import jax.numpy as jnp
from jax import lax
from jax.experimental import pallas as pl
from jax.experimental.pallas import tpu as pltpu

F32 = jnp.float32
BF16 = jnp.bfloat16

GRID_W = 64
SCAN_BLOCK = 128
S5_GROUP = 16
S5_SUB = 16
ML_HEADS = 4
N_MIXERS = 3
LN_EPS = 1e-5
VMEM_LIMIT = 56 * 1024 * 1024


def _cparams(*sem):
    return pltpu.CompilerParams(dimension_semantics=sem, vmem_limit_bytes=VMEM_LIMIT)


def _const_spec(shape):
    nd = len(shape)
    return pl.BlockSpec(shape, lambda *_: (0,) * nd, pipeline_mode=pl.Buffered(1))


def _ln(x, g, b):
    mu = jnp.mean(x, axis=-1, keepdims=True)
    xc = x - mu
    var = jnp.mean(xc * xc, axis=-1, keepdims=True)
    return xc * lax.rsqrt(var + LN_EPS) * g + b


def _sigmoid(x):
    return 1.0 / (1.0 + jnp.exp(-x))


def _silu(x):
    return x * _sigmoid(x)


def _gelu(x):
    return 0.5 * x * (1.0 + jnp.tanh(math.sqrt(2.0 / math.pi) * (x + 0.044715 * (x * x * x))))


def _bdot(a, b):
    return jnp.dot(a.astype(BF16), b.astype(BF16), preferred_element_type=F32)


def _mod_kernel(c_ref, w_ref, b_ref, o_ref):
    o_ref[0] = _bdot(_silu(c_ref[...]), w_ref[0]) + b_ref[0]


def _modulation(c, c_ctx, mod_w, mod_b):
    depth, d, d6 = mod_w.shape
    bsz = c.shape[0]
    rows = 8
    cs = jnp.zeros((rows, d), F32).at[:bsz].set(c).at[bsz].set(c_ctx)
    tn = d6 // 6
    return pl.pallas_call(
        _mod_kernel,
        grid=(depth, d6 // tn),
        in_specs=[pl.BlockSpec((rows, d), lambda i, j: (0, 0)),
                  pl.BlockSpec((1, d, tn), lambda i, j: (i, 0, j)),
                  pl.BlockSpec((1, 1, tn), lambda i, j: (i, 0, j))],
        out_specs=pl.BlockSpec((1, rows, tn), lambda i, j: (i, 0, j)),
        out_shape=jax.ShapeDtypeStruct((depth, rows, d6), F32),
        compiler_params=_cparams("parallel", "parallel"),
        name="modulation",
    )(cs, mod_w, mod_b.reshape(depth, 1, d6))


def _ffn_kernel(*refs, width, tm, n_chunks, alpha, halo):
    if halo:
        top_ref, x_ref, bot_ref = refs[:3]
        refs = refs[3:]
    else:
        x_ref = refs[0]
        refs = refs[1:]
    mod_ref, wg_ref, wu_ref, cw_ref, cb_ref, wd_ref, lng_ref, lnb_ref, o_ref, xbuf, gbuf, acc = refs
    i = pl.program_id(1)
    nt = pl.num_programs(1)
    pad = 8
    hw = width if halo else 0
    fc = gbuf.shape[1]

    sh2 = mod_ref[0, 3:4, :]
    sc2 = 1.0 + mod_ref[0, 4:5, :]
    g2 = 1.0 + mod_ref[0, 5:6, :]
    x = x_ref[0]
    xbuf[hw:hw + tm, :] = (x * sc2 + sh2).astype(BF16)
    if halo:
        xbuf[0:hw, :] = (top_ref[0] * sc2 + sh2).astype(BF16)
        xbuf[hw + tm:, :] = (bot_ref[0] * sc2 + sh2).astype(BF16)
    acc[...] = jnp.zeros_like(acc)
    gbuf[0:pad, :] = jnp.zeros((pad, fc), F32)
    gbuf[pad + tm + 2 * hw:, :] = jnp.zeros((pad, fc), F32)

    col = lax.broadcasted_iota(jnp.int32, (tm, fc), 0) & (width - 1)
    not_first = col > 0
    not_last = col < width - 1

    def chunk(j, carry):
        gbuf[pad:pad + tm + 2 * hw, :] = jnp.dot(xbuf[...], wg_ref[j], preferred_element_type=F32)
        if halo:
            @pl.when(i == 0)
            def _():
                gbuf[pad:pad + hw, :] = jnp.zeros((hw, fc), F32)

            @pl.when(i == nt - 1)
            def _():
                gbuf[pad + hw + tm:pad + 2 * hw + tm, :] = jnp.zeros((hw, fc), F32)

        def taps(dc):
            t = None
            for dr in ((-1, 0, 1) if halo else (0,)):
                start = pad + hw + dr * width + dc
                w = cw_ref[j, 3 * (dr + 1) + dc + 1:3 * (dr + 1) + dc + 2, :]
                term = gbuf[start:start + tm, :] * w
                t = term if t is None else t + term
            return t

        gate = (taps(0) + jnp.where(not_first, taps(-1), 0.0) + jnp.where(not_last, taps(1), 0.0)
                + cb_ref[j])
        up = jnp.dot(xbuf[hw:hw + tm, :], wu_ref[j], preferred_element_type=F32)
        act = (_gelu(gate) * up).astype(BF16)
        acc[...] += jnp.dot(act, wd_ref[j], preferred_element_type=F32)
        return carry

    lax.fori_loop(0, n_chunks, chunk, 0)
    o_ref[0] = _ln(alpha * x + g2 * acc[...], lng_ref[...], lnb_ref[...])


def _ffn_chunk(f):
    for fc in (256, 128):
        if f % fc == 0:
            return fc
    raise ValueError(f"ffn hidden size {f} is not a multiple of 128")


def _conv_ffn_ln(h, mod, w_gate, w_up, conv_w, conv_b, w_down, ln_g, ln_b, *, width, alpha, mod_row=None):
    bsz, length, d = h.shape
    f = w_gate.shape[1]
    fc = _ffn_chunk(f)
    nf = f // fc
    rows = length // width
    halo = rows > 1
    tm = min(length, 512) if halo else length
    assert length % tm == 0 and tm % width == 0 and width & (width - 1) == 0
    nt = length // tm
    r = tm // width
    wg = w_gate.astype(BF16).reshape(d, nf, fc).transpose(1, 0, 2)
    wu = w_up.astype(BF16).reshape(d, nf, fc).transpose(1, 0, 2)
    wd = w_down.astype(BF16).reshape(nf, fc, d)
    cw = conv_w.reshape(9, nf, fc).transpose(1, 0, 2)
    cb = conv_b.reshape(nf, 1, fc)
    mod_map = (lambda b, i: (b, 0, 0)) if mod_row is None else (lambda b, i: (mod_row, 0, 0))
    x_spec = pl.BlockSpec((1, tm, d), lambda b, i: (b, i, 0))
    in_specs, args = [x_spec], [h]
    if halo:
        nrow = length // width
        in_specs = [pl.BlockSpec((1, width, d), lambda b, i: (b, jnp.maximum(i * r - 1, 0), 0)), x_spec,
                    pl.BlockSpec((1, width, d), lambda b, i: (b, jnp.minimum((i + 1) * r, nrow - 1), 0))]
        args = [h, h, h]
    in_specs += [pl.BlockSpec((1, 8, d), mod_map), _const_spec(wg.shape), _const_spec(wu.shape),
                 _const_spec(cw.shape), _const_spec(cb.shape), _const_spec(wd.shape),
                 _const_spec((1, d)), _const_spec((1, d))]
    args += [mod, wg, wu, cw, cb, wd, ln_g.reshape(1, d), ln_b.reshape(1, d)]
    hw = width if halo else 0
    return pl.pallas_call(
        functools.partial(_ffn_kernel, width=width, tm=tm, n_chunks=nf, alpha=alpha, halo=halo),
        grid=(bsz, nt),
        in_specs=in_specs,
        out_specs=pl.BlockSpec((1, tm, d), lambda b, i: (b, i, 0)),
        out_shape=jax.ShapeDtypeStruct((bsz, length, d), F32),
        scratch_shapes=[pltpu.VMEM((tm + 2 * hw, d), BF16),
                        pltpu.VMEM((tm + 2 * hw + 16, fc), F32),
                        pltpu.VMEM((tm, d), F32)],
        compiler_params=_cparams("parallel", "parallel"),
        name="conv_ffn_ln",
    )(*args)


def _jln(x, g, b):
    mu = x.mean(-1, keepdims=True)
    var = jnp.square(x - mu).mean(-1, keepdims=True)
    return (x - mu) * lax.rsqrt(var + LN_EPS) * g + b


def _j_dwconv1d(x, w, b):
    k = w.shape[0]
    y = lax.conv_general_dilated(x, w[:, None, :], (1,), [((k - 1) // 2, k // 2)],
                                 dimension_numbers=('NWC', 'WIO', 'NWC'), feature_group_count=x.shape[-1])
    return y + b


def _j_s5_discretise(lam_re, lam_im, log_dt, b_re, b_im):
    lre = jnp.minimum(lam_re, -1e-4)
    lim = lam_im
    dt = jnp.exp(log_dt)[:, None]
    mag = jnp.exp(lre * dt)
    lb_re, lb_im = mag * jnp.cos(lim * dt), mag * jnp.sin(lim * dt)
    nr, ni = lb_re - 1.0, lb_im
    den = lre * lre + lim * lim
    cr = (nr * lre + ni * lim) / den
    ci = (ni * lre - nr * lim) / den
    bb_re = cr[..., None] * b_re - ci[..., None] * b_im
    bb_im = cr[..., None] * b_im + ci[..., None] * b_re
    return lb_re, lb_im, bb_re, bb_im


def _j_combine(e1, e2):
    a1r, a1i, b1r, b1i = e1
    a2r, a2i, b2r, b2i = e2
    return (a2r * a1r - a2i * a1i, a2r * a1i + a2i * a1r,
            a2r * b1r - a2i * b1i + b2r, a2r * b1i + a2i * b1r + b2i)


def _j_s5_scan(u, lb_re, lb_im, bb_re, bb_im, c_re, c_im, s0):
    bsz, length, d = u.shape
    groups = d // S5_GROUP
    nblk = length // 128
    ub = jnp.moveaxis(u.reshape(bsz, nblk, 128, groups, S5_GROUP), 1, 0)

    def step(carry, u_blk):
        sr, si = carry
        bu_re = jnp.einsum('btgs,gps->btgp', u_blk, bb_re)
        bu_im = jnp.einsum('btgs,gps->btgp', u_blk, bb_im)
        a_re = jnp.broadcast_to(lb_re, bu_re.shape)
        a_im = jnp.broadcast_to(lb_im, bu_im.shape)
        pa_re, pa_im, pb_re, pb_im = lax.associative_scan(_j_combine, (a_re, a_im, bu_re, bu_im), axis=1)
        st_re = pa_re * sr[:, None] - pa_im * si[:, None] + pb_re
        st_im = pa_re * si[:, None] + pa_im * sr[:, None] + pb_im
        y = jnp.einsum('gsp,btgp->btgs', c_re, st_re) - jnp.einsum('gsp,btgp->btgs', c_im, st_im)
        return (st_re[:, -1], st_im[:, -1]), y.reshape(bsz, 128, d)

    s_end, ys = lax.scan(step, s0, ub)
    return jnp.moveaxis(ys, 0, 1).reshape(bsz, length, d), s_end


def _j_s5_mixer(ul, uc, lam_re, lam_im, log_dt, b_re, b_im, c_re, c_im, d_skip, w_glu, b_glu):
    y_lat, y_ctx = d_skip * ul, d_skip * uc
    bsz = ul.shape[0]
    groups, p = lam_re.shape[1], lam_re.shape[2]
    for direction in range(2):
        lb_re, lb_im, bb_re, bb_im = _j_s5_discretise(lam_re[direction], lam_im[direction], log_dt[direction],
                                                      b_re[direction], b_im[direction])
        cr, ci = c_re[direction], c_im[direction]
        zero = jnp.zeros((bsz, groups, p), F32)
        rev = (lambda a: a[:, ::-1]) if direction == 1 else (lambda a: a)
        yc, s_ctx = _j_s5_scan(rev(uc), lb_re, lb_im, bb_re, bb_im, cr, ci, (zero, zero))
        yl, _ = _j_s5_scan(rev(ul), lb_re, lb_im, bb_re, bb_im, cr, ci, s_ctx)
        y_ctx = y_ctx + rev(yc)
        y_lat = y_lat + rev(yl)

    def glu(y):
        z = jax.nn.gelu(y) @ w_glu + b_glu
        val, gate = jnp.split(z, 2, axis=-1)
        return val * jax.nn.sigmoid(gate)

    return glu(y_lat), glu(y_ctx)


def _j_mlstm_scan(q, k, v, ig, lf, state):
    bsz, nh, length, dh = q.shape
    nblk = length // SCAN_BLOCK
    blocks = lambda a: jnp.moveaxis(a.reshape(bsz, nh, nblk, SCAN_BLOCK, *a.shape[3:]), 2, 0)
    tri = jnp.tril(jnp.ones((SCAN_BLOCK, SCAN_BLOCK), bool))

    def step(carry, blk):
        C, n, m = carry
        qb, kb, vb, ib, fb = blk
        b = jnp.cumsum(fb, axis=-1)
        logd = jnp.where(tri, b[..., :, None] - b[..., None, :] + ib[..., None, :], -jnp.inf)
        m_inter = b + m[..., None]
        m_t = jnp.maximum(m_inter, logd.max(-1))
        w_inter = jnp.exp(m_inter - m_t)
        s = jnp.exp(logd - m_t[..., None]) * jnp.einsum('bhtd,bhsd->bhts', qb, kb)
        num = w_inter[..., None] * jnp.einsum('bhed,bhtd->bhte', C, qb) + jnp.einsum('bhts,bhse->bhte', s, vb)
        den = w_inter * jnp.einsum('bhd,bhtd->bht', n, qb) + s.sum(-1)
        h = num / jnp.maximum(jnp.abs(den), jnp.exp(-m_t))[..., None]
        decay = b[..., -1:] - b + ib
        m_new = jnp.maximum(b[..., -1] + m, decay.max(-1))
        w_prev = jnp.exp(b[..., -1] + m - m_new)
        w_r = jnp.exp(decay - m_new[..., None])
        C = w_prev[..., None, None] * C + jnp.einsum('bhse,bhsd->bhed', w_r[..., None] * vb, kb)
        n = w_prev[..., None] * n + jnp.einsum('bhs,bhsd->bhd', w_r, kb)
        return (C, n, m_new), h

    state, hs = lax.scan(step, state, tuple(blocks(a) for a in (q, k, v, ig, lf)))
    return jnp.moveaxis(hs, 0, 2).reshape(bsz, nh, length, dh), state


def _j_mlstm_project(u, w_up, conv_w, conv_b, w_q, w_k, w_v, w_gates, b_gates):
    bsz, length, _ = u.shape
    e = w_q.shape[0]
    dh = e // ML_HEADS
    xm, z = jnp.split(u @ w_up, 2, axis=-1)
    xc = jax.nn.silu(_j_dwconv1d(xm, conv_w, conv_b))
    q = xc @ w_q
    k = (xc @ w_k) * dh ** -0.5
    v = xm @ w_v
    g = q @ w_gates[0] + k @ w_gates[1] + v @ w_gates[2] + b_gates
    heads = lambda a: a.reshape(bsz, length, ML_HEADS, dh).transpose(0, 2, 1, 3)
    return heads(q), heads(k), heads(v), g.reshape(bsz, length, 2, 2, ML_HEADS), xc, z


def _j_mlstm_out(h, xc, z, gn_g, skip, w_down):
    bsz, nh, length, dh = h.shape
    mu = h.mean(-1, keepdims=True)
    var = jnp.square(h - mu).mean(-1, keepdims=True)
    hn = ((h - mu) * lax.rsqrt(var + LN_EPS)).transpose(0, 2, 1, 3).reshape(bsz, length, nh * dh)
    hn = hn * gn_g + skip * xc
    return (hn * jax.nn.silu(z)) @ w_down


def _j_mlstm_mixer(u_lat, u_ctx, w_up, conv_w, conv_b, w_q, w_k, w_v, w_gates, b_gates, gn_g, skip, w_down):
    lat = _j_mlstm_project(u_lat, w_up, conv_w, conv_b, w_q, w_k, w_v, w_gates, b_gates)
    cxt = _j_mlstm_project(u_ctx, w_up, conv_w, conv_b, w_q, w_k, w_v, w_gates, b_gates)
    bsz = u_lat.shape[0]
    dh = w_q.shape[0] // ML_HEADS
    h_lat = jnp.zeros(lat[0].shape, F32)
    h_ctx = jnp.zeros(cxt[0].shape, F32)
    for direction in range(2):
        rev = (lambda a: jnp.flip(a, axis=2)) if direction == 1 else (lambda a: a)

        def prep(p):
            q, k, v, g = p[:4]
            ig = jnp.moveaxis(g[:, :, direction, 0], 1, 2)
            lf = jax.nn.log_sigmoid(jnp.moveaxis(g[:, :, direction, 1], 1, 2))
            return tuple(rev(a) for a in (q, k, v, ig, lf))

        st0 = (jnp.zeros((bsz, ML_HEADS, dh, dh), F32), jnp.zeros((bsz, ML_HEADS, dh), F32),
               jnp.zeros((bsz, ML_HEADS), F32))
        hc, st_ctx = _j_mlstm_scan(*prep(cxt), st0)
        hl, _ = _j_mlstm_scan(*prep(lat), st_ctx)
        h_ctx = h_ctx + rev(hc)
        h_lat = h_lat + rev(hl)
    return (_j_mlstm_out(h_lat, lat[4], lat[5], gn_g, skip, w_down),
            _j_mlstm_out(h_ctx, cxt[4], cxt[5], gn_g, skip, w_down))


def _j_conformer(u, w_in, b_in, dw_w, dw_b, ln_g, ln_b, w_out, b_out):
    a, gate = jnp.split(u @ w_in + b_in, 2, axis=-1)
    hmid = a * jax.nn.sigmoid(gate)
    hmid = jax.nn.silu(_jln(_j_dwconv1d(hmid, dw_w, dw_b), ln_g, ln_b))
    return hmid @ w_out + b_out


def _grid_transpose(x, rows, cols):
    bsz, _, ch = x.shape
    return x.reshape(bsz, rows, cols, ch).transpose(0, 2, 1, 3).reshape(bsz, rows * cols, ch)


def kernel(x, c, ctx, c_ctx, mod_w, mod_b, post_ln_g, post_ln_b, ffn_w_gate, ffn_w_up, ffn_conv_w, ffn_conv_b, ffn_w_down, s5_lambda_re, s5_lambda_im, s5_log_dt, s5_b_re, s5_b_im, s5_c_re, s5_c_im, s5_d, s5_w_glu, s5_b_glu, ml_w_up, ml_conv_w, ml_conv_b, ml_w_q, ml_w_k, ml_w_v, ml_w_gates, ml_b_gates, ml_gn_g, ml_skip, ml_w_down, cv_w_in, cv_b_in, cv_dw_w, cv_dw_b, cv_ln_g, cv_ln_b, cv_w_out, cv_b_out):
    bsz, length, d = x.shape
    depth = mod_w.shape[0]
    ctx_len = ctx.shape[1]
    rows = length // GRID_W
    alpha = (2 * depth) ** 0.25

    modv = _modulation(c, c_ctx, mod_w, mod_b)
    pad2 = jnp.zeros((2, d), F32)
    h_lat, h_ctx = x, ctx
    for i in range(depth):
        kind, occ = i % N_MIXERS, i // N_MIXERS
        last = i == depth - 1
        mod_l = jnp.concatenate([modv[i, :bsz].reshape(bsz, 6, d), jnp.zeros((bsz, 2, d), F32)], axis=1)
        mod_c = jnp.concatenate([modv[i, bsz].reshape(6, d), pad2], axis=0)[None]
        u_lat = h_lat * (1 + mod_l[:, 1:2]) + mod_l[:, 0:1]
        u_ctx = h_ctx * (1 + mod_c[:, 1:2]) + mod_c[:, 0:1]
        col_major = (kind != 2) and (occ % 2 == 1)
        if col_major:
            u_lat = _grid_transpose(u_lat, rows, GRID_W)
        if kind == 0:
            y_lat, y_ctx = _j_s5_mixer(u_lat, u_ctx, s5_lambda_re[occ], s5_lambda_im[occ], s5_log_dt[occ],
                                       s5_b_re[occ], s5_b_im[occ], s5_c_re[occ], s5_c_im[occ], s5_d[occ],
                                       s5_w_glu[occ], s5_b_glu[occ])
        elif kind == 1:
            y_lat, y_ctx = _j_mlstm_mixer(u_lat, u_ctx, ml_w_up[occ], ml_conv_w[occ], ml_conv_b[occ], ml_w_q[occ],
                                          ml_w_k[occ], ml_w_v[occ], ml_w_gates[occ], ml_b_gates[occ], ml_gn_g[occ],
                                          ml_skip[occ], ml_w_down[occ])
        else:
            cv = (cv_w_in[occ], cv_b_in[occ], cv_dw_w[occ], cv_dw_b[occ], cv_ln_g[occ], cv_ln_b[occ],
                  cv_w_out[occ], cv_b_out[occ])
            y_lat = _j_conformer(u_lat, *cv)
            y_ctx = _j_conformer(u_ctx, *cv)
        if col_major:
            y_lat = _grid_transpose(y_lat, GRID_W, rows)
        h_lat = _jln(alpha * h_lat + (1 + mod_l[:, 2:3]) * y_lat, post_ln_g[i, 0], post_ln_b[i, 0])
        ffn = (ffn_w_gate[i], ffn_w_up[i], ffn_conv_w[i], ffn_conv_b[i], ffn_w_down[i],
               post_ln_g[i, 1], post_ln_b[i, 1])
        h_lat = _conv_ffn_ln(h_lat, mod_l, *ffn, width=GRID_W, alpha=alpha)
        if not last:
            h_ctx = _jln(alpha * h_ctx + (1 + mod_c[:, 2:3]) * y_ctx, post_ln_g[i, 0], post_ln_b[i, 0])
            h_ctx = _conv_ffn_ln(h_ctx, mod_c, *ffn, width=ctx_len, alpha=alpha, mod_row=0)
    return h_lat
```

```python
import functools
import math

import jax
import jax.numpy as jnp
from jax import lax
from jax.experimental import pallas as pl
from jax.experimental.pallas import tpu as pltpu

F32 = jnp.float32
BF16 = jnp.bfloat16

GRID_W = 64
SCAN_BLOCK = 128
S5_GROUP = 16
S5_SUB = 16
ML_HEADS = 4
N_MIXERS = 3
LN_EPS = 1e-5
VMEM_LIMIT = 56 * 1024 * 1024


def _cparams(*sem):
    return pltpu.CompilerParams(dimension_semantics=sem, vmem_limit_bytes=VMEM_LIMIT)


def _const_spec(shape):
    nd = len(shape)
    return pl.BlockSpec(shape, lambda *_: (0,) * nd, pipeline_mode=pl.Buffered(1))


def _ln(x, g, b):
    mu = jnp.mean(x, axis=-1, keepdims=True)
    xc = x - mu
    var = jnp.mean(xc * xc, axis=-1, keepdims=True)
    return xc * lax.rsqrt(var + LN_EPS) * g + b


def _sigmoid(x):
    return 1.0 / (1.0 + jnp.exp(-x))


def _silu(x):
    return x * _sigmoid(x)


def _gelu(x):
    return 0.5 * x * (1.0 + jnp.tanh(math.sqrt(2.0 / math.pi) * (x + 0.044715 * (x * x * x))))


def _bdot(a, b):
    return jnp.dot(a.astype(BF16), b.astype(BF16), preferred_element_type=F32)


def _mod_kernel(c_ref, w_ref, b_ref, o_ref):
    o_ref[0] = _bdot(_silu(c_ref[...]), w_ref[0]) + b_ref[0]


def _modulation(c, c_ctx, mod_w, mod_b):
    depth, d, d6 = mod_w.shape
    bsz = c.shape[0]
    rows = 8
    cs = jnp.zeros((rows, d), F32).at[:bsz].set(c).at[bsz].set(c_ctx)
    tn = d6 // 6
    return pl.pallas_call(
        _mod_kernel,
        grid=(depth, d6 // tn),
        in_specs=[pl.BlockSpec((rows, d), lambda i, j: (0, 0)),
                  pl.BlockSpec((1, d, tn), lambda i, j: (i, 0, j)),
                  pl.BlockSpec((1, 1, tn), lambda i, j: (i, 0, j))],
        out_specs=pl.BlockSpec((1, rows, tn), lambda i, j: (i, 0, j)),
        out_shape=jax.ShapeDtypeStruct((depth, rows, d6), F32),
        compiler_params=_cparams("parallel", "parallel"),
        name="modulation",
    )(cs, mod_w, mod_b.reshape(depth, 1, d6))


def _ffn_kernel(*refs, width, tm, n_chunks, alpha, halo):
    if halo:
        top_ref, x_ref, bot_ref = refs[:3]
        refs = refs[3:]
    else:
        x_ref = refs[0]
        refs = refs[1:]
    mod_ref, wg_ref, wu_ref, cw_ref, cb_ref, wd_ref, lng_ref, lnb_ref, o_ref, xbuf, gbuf, acc = refs
    i = pl.program_id(1)
    nt = pl.num_programs(1)
    pad = 8
    hw = width if halo else 0
    fc = gbuf.shape[1]

    sh2 = mod_ref[0, 3:4, :]
    sc2 = 1.0 + mod_ref[0, 4:5, :]
    g2 = 1.0 + mod_ref[0, 5:6, :]
    x = x_ref[0]
    xbuf[hw:hw + tm, :] = (x * sc2 + sh2).astype(BF16)
    if halo:
        xbuf[0:hw, :] = (top_ref[0] * sc2 + sh2).astype(BF16)
        xbuf[hw + tm:, :] = (bot_ref[0] * sc2 + sh2).astype(BF16)
    acc[...] = jnp.zeros_like(acc)
    gbuf[0:pad, :] = jnp.zeros((pad, fc), F32)
    gbuf[pad + tm + 2 * hw:, :] = jnp.zeros((pad, fc), F32)

    col = lax.broadcasted_iota(jnp.int32, (tm, fc), 0) & (width - 1)
    not_first = col > 0
    not_last = col < width - 1

    def chunk(j, carry):
        gbuf[pad:pad + tm + 2 * hw, :] = jnp.dot(xbuf[...], wg_ref[j], preferred_element_type=F32)
        if halo:
            @pl.when(i == 0)
            def _():
                gbuf[pad:pad + hw, :] = jnp.zeros((hw, fc), F32)

            @pl.when(i == nt - 1)
            def _():
                gbuf[pad + hw + tm:pad + 2 * hw + tm, :] = jnp.zeros((hw, fc), F32)

        def taps(dc):
            t = None
            for dr in ((-1, 0, 1) if halo else (0,)):
                start = pad + hw + dr * width + dc
                w = cw_ref[j, 3 * (dr + 1) + dc + 1:3 * (dr + 1) + dc + 2, :]
                term = gbuf[start:start + tm, :] * w
                t = term if t is None else t + term
            return t

        gate = (taps(0) + jnp.where(not_first, taps(-1), 0.0) + jnp.where(not_last, taps(1), 0.0)
                + cb_ref[j])
        up = jnp.dot(xbuf[hw:hw + tm, :], wu_ref[j], preferred_element_type=F32)
        act = (_gelu(gate) * up).astype(BF16)
        acc[...] += jnp.dot(act, wd_ref[j], preferred_element_type=F32)
        return carry

    lax.fori_loop(0, n_chunks, chunk, 0)
    o_ref[0] = _ln(alpha * x + g2 * acc[...], lng_ref[...], lnb_ref[...])


def _ffn_chunk(f):
    for fc in (256, 128):
        if f % fc == 0:
            return fc
    raise ValueError(f"ffn hidden size {f} is not a multiple of 128")


def _conv_ffn_ln(h, mod, w_gate, w_up, conv_w, conv_b, w_down, ln_g, ln_b, *, width, alpha, mod_row=None):
    bsz, length, d = h.shape
    f = w_gate.shape[1]
    fc = _ffn_chunk(f)
    nf = f // fc
    rows = length // width
    halo = rows > 1
    tm = min(length, 512) if halo else length
    assert length % tm == 0 and tm % width == 0 and width & (width - 1) == 0
    nt = length // tm
    r = tm // width
    wg = w_gate.astype(BF16).reshape(d, nf, fc).transpose(1, 0, 2)
    wu = w_up.astype(BF16).reshape(d, nf, fc).transpose(1, 0, 2)
    wd = w_down.astype(BF16).reshape(nf, fc, d)
    cw = conv_w.reshape(9, nf, fc).transpose(1, 0, 2)
    cb = conv_b.reshape(nf, 1, fc)
    mod_map = (lambda b, i: (b, 0, 0)) if mod_row is None else (lambda b, i: (mod_row, 0, 0))
    x_spec = pl.BlockSpec((1, tm, d), lambda b, i: (b, i, 0))
    in_specs, args = [x_spec], [h]
    if halo:
        nrow = length // width
        in_specs = [pl.BlockSpec((1, width, d), lambda b, i: (b, jnp.maximum(i * r - 1, 0), 0)), x_spec,
                    pl.BlockSpec((1, width, d), lambda b, i: (b, jnp.minimum((i + 1) * r, nrow - 1), 0))]
        args = [h, h, h]
    in_specs += [pl.BlockSpec((1, 8, d), mod_map), _const_spec(wg.shape), _const_spec(wu.shape),
                 _const_spec(cw.shape), _const_spec(cb.shape), _const_spec(wd.shape),
                 _const_spec((1, d)), _const_spec((1, d))]
    args += [mod, wg, wu, cw, cb, wd, ln_g.reshape(1, d), ln_b.reshape(1, d)]
    hw = width if halo else 0
    return pl.pallas_call(
        functools.partial(_ffn_kernel, width=width, tm=tm, n_chunks=nf, alpha=alpha, halo=halo),
        grid=(bsz, nt),
        in_specs=in_specs,
        out_specs=pl.BlockSpec((1, tm, d), lambda b, i: (b, i, 0)),
        out_shape=jax.ShapeDtypeStruct((bsz, length, d), F32),
        scratch_shapes=[pltpu.VMEM((tm + 2 * hw, d), BF16),
                        pltpu.VMEM((tm + 2 * hw + 16, fc), F32),
                        pltpu.VMEM((tm, d), F32)],
        compiler_params=_cparams("parallel", "parallel"),
        name="conv_ffn_ln",
    )(*args)


def _s5_operators(lam_re, lam_im, log_dt, b_re, b_im, c_re, c_im):
    hi = lax.Precision.HIGHEST
    s = S5_SUB
    e_cols, f_rows, m_tot, a_rows = [], [], None, []
    for direction in range(2):
        lre = jnp.minimum(lam_re[direction], -1e-4)
        lim = lam_im[direction]
        dt = jnp.exp(log_dt[direction])[:, None]
        mag = jnp.exp(lre * dt)
        ar, ai = mag * jnp.cos(lim * dt), mag * jnp.sin(lim * dt)
        nr, ni = ar - 1.0, ai
        den = lre * lre + lim * lim
        cr = (nr * lre + ni * lim) / den
        ci = (ni * lre - nr * lim) / den
        bbr = cr[..., None] * b_re[direction] - ci[..., None] * b_im[direction]
        bbi = cr[..., None] * b_im[direction] + ci[..., None] * b_re[direction]
        pr, pi = [jnp.ones_like(ar)], [jnp.zeros_like(ai)]
        for _ in range(s):
            pr.append(pr[-1] * ar - pi[-1] * ai)
            pi.append(pr[-2] * ai + pi[-1] * ar)
        pr, pi = jnp.stack(pr), jnp.stack(pi)
        abr = pr[:s, :, :, None] * bbr - pi[:s, :, :, None] * bbi
        abi = pr[:s, :, :, None] * bbi + pi[:s, :, :, None] * bbr
        ccr, cci = c_re[direction], c_im[direction]
        kern = (jnp.einsum('gop,tgpi->tgoi', ccr, abr, precision=hi)
                - jnp.einsum('gop,tgpi->tgoi', cci, abi, precision=hi))
        j = jnp.arange(s)[:, None]
        t = jnp.arange(s)[None, :]
        lag = (t - j) if direction == 0 else (j - t)
        m = jnp.where((lag >= 0)[:, :, None, None, None], kern[jnp.clip(lag, 0, s - 1)], 0.0)
        m = m.transpose(2, 0, 4, 1, 3)
        m_tot = m if m_tot is None else m_tot + m
        order = (s - 1 - jnp.arange(s)) if direction == 0 else jnp.arange(s)
        er = abr[order].transpose(1, 0, 3, 2)
        ei = abi[order].transpose(1, 0, 3, 2)
        e_cols.append((er, ei))
        expo = (jnp.arange(s) + 1) if direction == 0 else (s - jnp.arange(s))
        zr, zi = pr[expo], pi[expo]
        f_re = jnp.einsum('gop,tgp->gpto', ccr, zr) - jnp.einsum('gop,tgp->gpto', cci, zi)
        f_im = -(jnp.einsum('gop,tgp->gpto', ccr, zi) + jnp.einsum('gop,tgp->gpto', cci, zr))
        f_rows.append((f_re, f_im))
        a_rows.append((jnp.concatenate([pr[s], pr[s]], -1), jnp.concatenate([-pi[s], pi[s]], -1)))
    g = lam_re.shape[1]
    w = s * S5_GROUP
    (fer, fei), (ber, bei) = e_cols
    e_mat = jnp.concatenate([fer, fei, ber, bei, fei, fer, bei, ber], axis=-1).reshape(g, w, -1)
    (ffr, ffi), (bfr, bfi) = f_rows
    f_mat = jnp.concatenate([ffr, ffi, bfr, bfi], axis=1).reshape(g, -1, w)
    a_mat = jnp.stack([a_rows[0][0], a_rows[0][1], a_rows[1][0], a_rows[1][1]], axis=1)
    return e_mat, m_tot.reshape(g, w, w), f_mat, a_mat


def _pick_group(res, n, gp, bsz):
    if gp == 1:
        return res
    sel = (lax.broadcasted_iota(jnp.int32, (res.shape[0], n), 0) & 7) // bsz
    out = res[:, :n]
    for k in range(1, gp):
        out = jnp.where(sel == k, res[:, k * n:(k + 1) * n], out)
    return out


def _s5_kernel(xc_ref, xl_ref, mc_ref, ml_ref, d_ref, a_ref, e_ref, m_ref, f_ref, yc_ref, yl_ref, se,
               *, bsz, gp, rc):
    rows_c, w = xc_ref.shape[1], xc_ref.shape[2]
    rows_l = xl_ref.shape[1]
    p2 = a_ref.shape[-1]

    def modulated(x_ref, mod_ref, r0, n):
        x = x_ref[0, pl.ds(r0, n), :].reshape(n // 8, 8, w)
        return (x * (1.0 + mod_ref[0, 1]) + mod_ref[0, 0]).reshape(n, w)

    def chunks(rows):
        return [(r0, min(rc, rows - r0)) for r0 in range(0, rows, rc)]

    segments = ((xc_ref, mc_ref, yc_ref, 0, rows_c), (xl_ref, ml_ref, yl_ref, rows_c, rows_l))

    for x_ref, mod_ref, _, base, rows in segments:
        for r0, n in chunks(rows):
            u = modulated(x_ref, mod_ref, r0, n).astype(BF16)
            e = jnp.dot(u, e_ref[0], preferred_element_type=F32)
            se[base + r0:base + r0 + n, :] = _pick_group(e, 4 * p2, gp, bsz)

    a1f, a2f, a1b, a2b = a_ref[0, 0], a_ref[0, 1], a_ref[0, 2], a_ref[0, 3]

    def make_step(base, tiles):
        def step(s, carry):
            vf, wf, vb, wb = carry
            rf = pl.multiple_of(base + s * 8, 8)
            rb = pl.multiple_of(base + (tiles - 1 - s) * 8, 8)
            ef_v = se[pl.ds(rf, 8), 0:p2]
            ef_w = se[pl.ds(rf, 8), 2 * p2:3 * p2]
            eb_v = se[pl.ds(rb, 8), p2:2 * p2]
            eb_w = se[pl.ds(rb, 8), 3 * p2:4 * p2]
            se[pl.ds(rf, 8), 0:p2] = vf
            se[pl.ds(rb, 8), p2:2 * p2] = vb
            return (a1f * vf + a2f * wf + ef_v, a1f * wf - a2f * vf + ef_w,
                    a1b * vb + a2b * wb + eb_v, a1b * wb - a2b * vb + eb_w)
        return step

    zero = jnp.zeros((8, p2), F32)
    carry = (zero, zero, zero, zero)
    carry = lax.fori_loop(0, rows_c // 8, make_step(0, rows_c // 8), carry, unroll=2)
    lax.fori_loop(0, rows_l // 8, make_step(rows_c, rows_l // 8), carry, unroll=2)

    for x_ref, mod_ref, y_ref, base, rows in segments:
        for r0, n in chunks(rows):
            u = modulated(x_ref, mod_ref, r0, n)
            s_in = se[base + r0:base + r0 + n, 0:2 * p2].astype(BF16)
            y = (_pick_group(jnp.dot(u.astype(BF16), m_ref[0], preferred_element_type=F32), w, gp, bsz)
                 + _pick_group(jnp.dot(s_in, f_ref[0], preferred_element_type=F32), w, gp, bsz))
            y = y + (u.reshape(n // 8, 8, w) * d_ref[0]).reshape(n, w)
            y_ref[0, pl.ds(r0, n), :] = _gelu(y).astype(BF16)


def _s5_rows(h, gp, col_major_rows=None):
    bsz, length, d = h.shape
    gt = d // S5_GROUP // gp
    if col_major_rows is None:
        x = h.reshape(bsz, length // S5_SUB, S5_SUB, gt, gp, S5_GROUP).transpose(3, 1, 4, 0, 2, 5)
    else:
        rows, cols = col_major_rows, length // col_major_rows
        x = h.reshape(bsz, rows // S5_SUB, S5_SUB, cols, gt, gp, S5_GROUP).transpose(4, 3, 1, 5, 0, 2, 6)
    return x.reshape(gt, (length // S5_SUB) * gp * bsz, S5_SUB * S5_GROUP)


def _s5_unrows(y, bsz, length, gp, col_major_rows=None):
    gt = y.shape[0]
    d = gt * gp * S5_GROUP
    if col_major_rows is None:
        x = y.reshape(gt, length // S5_SUB, gp, bsz, S5_SUB, S5_GROUP).transpose(3, 1, 4, 0, 2, 5)
    else:
        rows, cols = col_major_rows, length // col_major_rows
        x = y.reshape(gt, cols, rows // S5_SUB, gp, bsz, S5_SUB, S5_GROUP).transpose(4, 2, 5, 1, 0, 3, 6)
    return x.reshape(bsz, length, d)


def _s5_pattern(v, gp, reps):
    d = v.shape[-1]
    gt = d // S5_GROUP // gp
    v = jnp.broadcast_to(v.reshape(-1, d), (reps, d))
    x = v.reshape(reps, gt, gp, 1, S5_GROUP).transpose(1, 2, 0, 3, 4)
    x = jnp.broadcast_to(x, (gt, gp, reps, S5_SUB, S5_GROUP))
    return x.reshape(gt, gp * reps, S5_SUB * S5_GROUP)


def _s5_core(h_lat, h_ctx, mod_l, mod_c, params, col_major):
    lam_re, lam_im, log_dt, b_re, b_im, c_re, c_im, d_skip = params
    bsz, length, d = h_lat.shape
    ctx_len = h_ctx.shape[1]
    assert 8 % bsz == 0 and (d // S5_GROUP) % (8 // bsz) == 0
    gp = 8 // bsz
    gt = d // S5_GROUP // gp
    w = S5_SUB * S5_GROUP
    rows_cm = (length // GRID_W) if col_major else None
    xl = _s5_rows(h_lat, gp, rows_cm)
    xc = _s5_rows(h_ctx, gp)
    e_mat, m_mat, f_mat, a_mat = _s5_operators(lam_re, lam_im, log_dt, b_re, b_im, c_re, c_im)
    p2 = a_mat.shape[-1]
    cat = lambda m: m.reshape(gt, gp, m.shape[1], m.shape[2]).transpose(0, 2, 1, 3).reshape(gt, m.shape[1], -1)
    e_cat, m_cat, f_cat = (cat(m).astype(BF16) for m in (e_mat, m_mat, f_mat))
    a_pat = jnp.broadcast_to(a_mat.reshape(gt, gp, 1, 4, p2), (gt, gp, bsz, 4, p2)).transpose(0, 3, 1, 2, 4)
    a_pat = a_pat.reshape(gt, 4, 8, p2)
    ml_pat = jnp.stack([_s5_pattern(mod_l[:, 0], gp, bsz), _s5_pattern(mod_l[:, 1], gp, bsz)], axis=1)
    mc_pat = jnp.stack([_s5_pattern(mod_c[:, 0], gp, bsz), _s5_pattern(mod_c[:, 1], gp, bsz)], axis=1)
    d_pat = _s5_pattern(d_skip, gp, bsz)
    rows_l, rows_c = xl.shape[1], xc.shape[1]
    rc = 512
    tile = lambda *shape: pl.BlockSpec((1,) + shape, lambda g: (g,) + (0,) * len(shape))
    yc, yl = pl.pallas_call(
        functools.partial(_s5_kernel, bsz=bsz, gp=gp, rc=rc),
        grid=(gt,),
        in_specs=[tile(rows_c, w), tile(rows_l, w), tile(2, 8, w), tile(2, 8, w), tile(8, w), tile(4, 8, p2),
                  tile(w, gp * 4 * p2), tile(w, gp * w), tile(2 * p2, gp * w)],
        out_specs=[tile(rows_c, w), tile(rows_l, w)],
        out_shape=[jax.ShapeDtypeStruct((gt, rows_c, w), BF16), jax.ShapeDtypeStruct((gt, rows_l, w), BF16)],
        scratch_shapes=[pltpu.VMEM((rows_c + rows_l, 4 * p2), F32)],
        compiler_params=_cparams("parallel"),
        name="s5_scan",
    )(xc, xl, mc_pat, ml_pat, d_pat, a_pat, e_cat, m_cat, f_cat)
    return _s5_unrows(yl, bsz, length, gp, rows_cm), _s5_unrows(yc, bsz, ctx_len, gp)


def _glu_ln_kernel(y_ref, h_ref, mod_ref, w_ref, b_ref, lng_ref, lnb_ref, o_ref, *, alpha):
    d = h_ref.shape[-1]
    z = jnp.dot(y_ref[0], w_ref[...], preferred_element_type=F32) + b_ref[...]
    glu = z[:, :d] * _sigmoid(z[:, d:])
    g1 = 1.0 + mod_ref[0, 2:3, :]
    o_ref[0] = _ln(alpha * h_ref[0] + g1 * glu, lng_ref[...], lnb_ref[...])


def _glu_ln(y, h, mod, w_glu, b_glu, ln_g, ln_b, *, alpha, mod_row=None):
    bsz, length, d = h.shape
    tm = min(length, 512)
    mod_map = (lambda b, i: (b, 0, 0)) if mod_row is None else (lambda b, i: (mod_row, 0, 0))
    tok = lambda: pl.BlockSpec((1, tm, d), lambda b, i: (b, i, 0))
    return pl.pallas_call(
        functools.partial(_glu_ln_kernel, alpha=alpha),
        grid=(bsz, length // tm),
        in_specs=[tok(), tok(), pl.BlockSpec((1, 8, d), mod_map), _const_spec((d, 2 * d)), _const_spec((1, 2 * d)),
                  _const_spec((1, d)), _const_spec((1, d))],
        out_specs=tok(),
        out_shape=jax.ShapeDtypeStruct((bsz, length, d), F32),
        compiler_params=_cparams("parallel", "parallel"),
        name="glu_ln",
    )(y, h, mod, w_glu.astype(BF16), b_glu.reshape(1, 2 * d), ln_g.reshape(1, d), ln_b.reshape(1, d))


def _jln(x, g, b):
    mu = x.mean(-1, keepdims=True)
    var = jnp.square(x - mu).mean(-1, keepdims=True)
    return (x - mu) * lax.rsqrt(var + LN_EPS) * g + b


def _j_dwconv1d(x, w, b):
    k = w.shape[0]
    y = lax.conv_general_dilated(x, w[:, None, :], (1,), [((k - 1) // 2, k // 2)],
                                 dimension_numbers=('NWC', 'WIO', 'NWC'), feature_group_count=x.shape[-1])
    return y + b


def _j_s5_discretise(lam_re, lam_im, log_dt, b_re, b_im):
    lre = jnp.minimum(lam_re, -1e-4)
    lim = lam_im
    dt = jnp.exp(log_dt)[:, None]
    mag = jnp.exp(lre * dt)
    lb_re, lb_im = mag * jnp.cos(lim * dt), mag * jnp.sin(lim * dt)
    nr, ni = lb_re - 1.0, lb_im
    den = lre * lre + lim * lim
    cr = (nr * lre + ni * lim) / den
    ci = (ni * lre - nr * lim) / den
    bb_re = cr[..., None] * b_re - ci[..., None] * b_im
    bb_im = cr[..., None] * b_im + ci[..., None] * b_re
    return lb_re, lb_im, bb_re, bb_im


def _j_combine(e1, e2):
    a1r, a1i, b1r, b1i = e1
    a2r, a2i, b2r, b2i = e2
    return (a2r * a1r - a2i * a1i, a2r * a1i + a2i * a1r,
            a2r * b1r - a2i * b1i + b2r, a2r * b1i + a2i * b1r + b2i)


def _j_s5_scan(u, lb_re, lb_im, bb_re, bb_im, c_re, c_im, s0):
    bsz, length, d = u.shape
    groups = d // S5_GROUP
    nblk = length // 128
    ub = jnp.moveaxis(u.reshape(bsz, nblk, 128, groups, S5_GROUP), 1, 0)

    def step(carry, u_blk):
        sr, si = carry
        bu_re = jnp.einsum('btgs,gps->btgp', u_blk, bb_re)
        bu_im = jnp.einsum('btgs,gps->btgp', u_blk, bb_im)
        a_re = jnp.broadcast_to(lb_re, bu_re.shape)
        a_im = jnp.broadcast_to(lb_im, bu_im.shape)
        pa_re, pa_im, pb_re, pb_im = lax.associative_scan(_j_combine, (a_re, a_im, bu_re, bu_im), axis=1)
        st_re = pa_re * sr[:, None] - pa_im * si[:, None] + pb_re
        st_im = pa_re * si[:, None] + pa_im * sr[:, None] + pb_im
        y = jnp.einsum('gsp,btgp->btgs', c_re, st_re) - jnp.einsum('gsp,btgp->btgs', c_im, st_im)
        return (st_re[:, -1], st_im[:, -1]), y.reshape(bsz, 128, d)

    s_end, ys = lax.scan(step, s0, ub)
    return jnp.moveaxis(ys, 0, 1).reshape(bsz, length, d), s_end


def _j_s5_mixer(ul, uc, lam_re, lam_im, log_dt, b_re, b_im, c_re, c_im, d_skip, w_glu, b_glu):
    y_lat, y_ctx = d_skip * ul, d_skip * uc
    bsz = ul.shape[0]
    groups, p = lam_re.shape[1], lam_re.shape[2]
    for direction in range(2):
        lb_re, lb_im, bb_re, bb_im = _j_s5_discretise(lam_re[direction], lam_im[direction], log_dt[direction],
                                                      b_re[direction], b_im[direction])
        cr, ci = c_re[direction], c_im[direction]
        zero = jnp.zeros((bsz, groups, p), F32)
        rev = (lambda a: a[:, ::-1]) if direction == 1 else (lambda a: a)
        yc, s_ctx = _j_s5_scan(rev(uc), lb_re, lb_im, bb_re, bb_im, cr, ci, (zero, zero))
        yl, _ = _j_s5_scan(rev(ul), lb_re, lb_im, bb_re, bb_im, cr, ci, s_ctx)
        y_ctx = y_ctx + rev(yc)
        y_lat = y_lat + rev(yl)

    def glu(y):
        z = jax.nn.gelu(y) @ w_glu + b_glu
        val, gate = jnp.split(z, 2, axis=-1)
        return val * jax.nn.sigmoid(gate)

    return glu(y_lat), glu(y_ctx)


def _j_mlstm_scan(q, k, v, ig, lf, state):
    bsz, nh, length, dh = q.shape
    nblk = length // SCAN_BLOCK
    blocks = lambda a: jnp.moveaxis(a.reshape(bsz, nh, nblk, SCAN_BLOCK, *a.shape[3:]), 2, 0)
    tri = jnp.tril(jnp.ones((SCAN_BLOCK, SCAN_BLOCK), bool))

    def step(carry, blk):
        C, n, m = carry
        qb, kb, vb, ib, fb = blk
        b = jnp.cumsum(fb, axis=-1)
        logd = jnp.where(tri, b[..., :, None] - b[..., None, :] + ib[..., None, :], -jnp.inf)
        m_inter = b + m[..., None]
        m_t = jnp.maximum(m_inter, logd.max(-1))
        w_inter = jnp.exp(m_inter - m_t)
        s = jnp.exp(logd - m_t[..., None]) * jnp.einsum('bhtd,bhsd->bhts', qb, kb)
        num = w_inter[..., None] * jnp.einsum('bhed,bhtd->bhte', C, qb) + jnp.einsum('bhts,bhse->bhte', s, vb)
        den = w_inter * jnp.einsum('bhd,bhtd->bht', n, qb) + s.sum(-1)
        h = num / jnp.maximum(jnp.abs(den), jnp.exp(-m_t))[..., None]
        decay = b[..., -1:] - b + ib
        m_new = jnp.maximum(b[..., -1] + m, decay.max(-1))
        w_prev = jnp.exp(b[..., -1] + m - m_new)
        w_r = jnp.exp(decay - m_new[..., None])
        C = w_prev[..., None, None] * C + jnp.einsum('bhse,bhsd->bhed', w_r[..., None] * vb, kb)
        n = w_prev[..., None] * n + jnp.einsum('bhs,bhsd->bhd', w_r, kb)
        return (C, n, m_new), h

    state, hs = lax.scan(step, state, tuple(blocks(a) for a in (q, k, v, ig, lf)))
    return jnp.moveaxis(hs, 0, 2).reshape(bsz, nh, length, dh), state


def _j_mlstm_project(u, w_up, conv_w, conv_b, w_q, w_k, w_v, w_gates, b_gates):
    bsz, length, _ = u.shape
    e = w_q.shape[0]
    dh = e // ML_HEADS
    xm, z = jnp.split(u @ w_up, 2, axis=-1)
    xc = jax.nn.silu(_j_dwconv1d(xm, conv_w, conv_b))
    q = xc @ w_q
    k = (xc @ w_k) * dh ** -0.5
    v = xm @ w_v
    g = q @ w_gates[0] + k @ w_gates[1] + v @ w_gates[2] + b_gates
    heads = lambda a: a.reshape(bsz, length, ML_HEADS, dh).transpose(0, 2, 1, 3)
    return heads(q), heads(k), heads(v), g.reshape(bsz, length, 2, 2, ML_HEADS), xc, z


def _j_mlstm_out(h, xc, z, gn_g, skip, w_down):
    bsz, nh, length, dh = h.shape
    mu = h.mean(-1, keepdims=True)
    var = jnp.square(h - mu).mean(-1, keepdims=True)
    hn = ((h - mu) * lax.rsqrt(var + LN_EPS)).transpose(0, 2, 1, 3).reshape(bsz, length, nh * dh)
    hn = hn * gn_g + skip * xc
    return (hn * jax.nn.silu(z)) @ w_down


def _j_mlstm_mixer(u_lat, u_ctx, w_up, conv_w, conv_b, w_q, w_k, w_v, w_gates, b_gates, gn_g, skip, w_down):
    lat = _j_mlstm_project(u_lat, w_up, conv_w, conv_b, w_q, w_k, w_v, w_gates, b_gates)
    cxt = _j_mlstm_project(u_ctx, w_up, conv_w, conv_b, w_q, w_k, w_v, w_gates, b_gates)
    bsz = u_lat.shape[0]
    dh = w_q.shape[0] // ML_HEADS
    h_lat = jnp.zeros(lat[0].shape, F32)
    h_ctx = jnp.zeros(cxt[0].shape, F32)
    for direction in range(2):
        rev = (lambda a: jnp.flip(a, axis=2)) if direction == 1 else (lambda a: a)

        def prep(p):
            q, k, v, g = p[:4]
            ig = jnp.moveaxis(g[:, :, direction, 0], 1, 2)
            lf = jax.nn.log_sigmoid(jnp.moveaxis(g[:, :, direction, 1], 1, 2))
            return tuple(rev(a) for a in (q, k, v, ig, lf))

        st0 = (jnp.zeros((bsz, ML_HEADS, dh, dh), F32), jnp.zeros((bsz, ML_HEADS, dh), F32),
               jnp.zeros((bsz, ML_HEADS), F32))
        hc, st_ctx = _j_mlstm_scan(*prep(cxt), st0)
        hl, _ = _j_mlstm_scan(*prep(lat), st_ctx)
        h_ctx = h_ctx + rev(hc)
        h_lat = h_lat + rev(hl)
    return (_j_mlstm_out(h_lat, lat[4], lat[5], gn_g, skip, w_down),
            _j_mlstm_out(h_ctx, cxt[4], cxt[5], gn_g, skip, w_down))


def _j_conformer(u, w_in, b_in, dw_w, dw_b, ln_g, ln_b, w_out, b_out):
    a, gate = jnp.split(u @ w_in + b_in, 2, axis=-1)
    hmid = a * jax.nn.sigmoid(gate)
    hmid = jax.nn.silu(_jln(_j_dwconv1d(hmid, dw_w, dw_b), ln_g, ln_b))
    return hmid @ w_out + b_out


def _grid_transpose(x, rows, cols):
    bsz, _, ch = x.shape
    return x.reshape(bsz, rows, cols, ch).transpose(0, 2, 1, 3).reshape(bsz, rows * cols, ch)


def kernel(x, c, ctx, c_ctx, mod_w, mod_b, post_ln_g, post_ln_b, ffn_w_gate, ffn_w_up, ffn_conv_w, ffn_conv_b, ffn_w_down, s5_lambda_re, s5_lambda_im, s5_log_dt, s5_b_re, s5_b_im, s5_c_re, s5_c_im, s5_d, s5_w_glu, s5_b_glu, ml_w_up, ml_conv_w, ml_conv_b, ml_w_q, ml_w_k, ml_w_v, ml_w_gates, ml_b_gates, ml_gn_g, ml_skip, ml_w_down, cv_w_in, cv_b_in, cv_dw_w, cv_dw_b, cv_ln_g, cv_ln_b, cv_w_out, cv_b_out):
    bsz, length, d = x.shape
    depth = mod_w.shape[0]
    ctx_len = ctx.shape[1]
    rows = length // GRID_W
    alpha = (2 * depth) ** 0.25

    modv = _modulation(c, c_ctx, mod_w, mod_b)
    pad2 = jnp.zeros((2, d), F32)
    h_lat, h_ctx = x, ctx
    for i in range(depth):
        kind, occ = i % N_MIXERS, i // N_MIXERS
        last = i == depth - 1
        mod_l = jnp.concatenate([modv[i, :bsz].reshape(bsz, 6, d), jnp.zeros((bsz, 2, d), F32)], axis=1)
        mod_c = jnp.concatenate([modv[i, bsz].reshape(6, d), pad2], axis=0)[None]
        col_major = (kind != 2) and (occ % 2 == 1)
        ln1 = (post_ln_g[i, 0], post_ln_b[i, 0])
        ffn = (ffn_w_gate[i], ffn_w_up[i], ffn_conv_w[i], ffn_conv_b[i], ffn_w_down[i],
               post_ln_g[i, 1], post_ln_b[i, 1])
        if kind == 0:
            s5 = (s5_lambda_re[occ], s5_lambda_im[occ], s5_log_dt[occ], s5_b_re[occ], s5_b_im[occ],
                  s5_c_re[occ], s5_c_im[occ], s5_d[occ])
            y_lat, y_ctx = _s5_core(h_lat, h_ctx, mod_l, mod_c, s5, col_major)
            h_lat = _glu_ln(y_lat, h_lat, mod_l, s5_w_glu[occ], s5_b_glu[occ], *ln1, alpha=alpha)
            h_lat = _conv_ffn_ln(h_lat, mod_l, *ffn, width=GRID_W, alpha=alpha)
            if not last:
                h_ctx = _glu_ln(y_ctx, h_ctx, mod_c, s5_w_glu[occ], s5_b_glu[occ], *ln1, alpha=alpha, mod_row=0)
                h_ctx = _conv_ffn_ln(h_ctx, mod_c, *ffn, width=ctx_len, alpha=alpha, mod_row=0)
            continue
        u_lat = h_lat * (1 + mod_l[:, 1:2]) + mod_l[:, 0:1]
        u_ctx = h_ctx * (1 + mod_c[:, 1:2]) + mod_c[:, 0:1]
        if col_major:
            u_lat = _grid_transpose(u_lat, rows, GRID_W)
        if kind == 1:
            y_lat, y_ctx = _j_mlstm_mixer(u_lat, u_ctx, ml_w_up[occ], ml_conv_w[occ], ml_conv_b[occ], ml_w_q[occ],
                                          ml_w_k[occ], ml_w_v[occ], ml_w_gates[occ], ml_b_gates[occ], ml_gn_g[occ],
                                          ml_skip[occ], ml_w_down[occ])
        else:
            cv = (cv_w_in[occ], cv_b_in[occ], cv_dw_w[occ], cv_dw_b[occ], cv_ln_g[occ], cv_ln_b[occ],
                  cv_w_out[occ], cv_b_out[occ])
            y_lat = _j_conformer(u_lat, *cv)
            y_ctx = _j_conformer(u_ctx, *cv)
        if col_major:
            y_lat = _grid_transpose(y_lat, GRID_W, rows)
        h_lat = _jln(alpha * h_lat + (1 + mod_l[:, 2:3]) * y_lat, post_ln_g[i, 0], post_ln_b[i, 0])
        ffn = (ffn_w_gate[i], ffn_w_up[i], ffn_conv_w[i], ffn_conv_b[i], ffn_w_down[i],
               post_ln_g[i, 1], post_ln_b[i, 1])
        h_lat = _conv_ffn_ln(h_lat, mod_l, *ffn, width=GRID_W, alpha=alpha)
        if not last:
            h_ctx = _jln(alpha * h_ctx + (1 + mod_c[:, 2:3]) * y_ctx, post_ln_g[i, 0], post_ln_b[i, 0])
            h_ctx = _conv_ffn_ln(h_ctx, mod_c, *ffn, width=ctx_len, alpha=alpha, mod_row=0)
    return h_lat
```

```python
import functools
import math

import jax
import jax.numpy as jnp
from jax import lax
from jax.experimental import pallas as pl
from jax.experimental.pallas import tpu as pltpu

F32 = jnp.float32
BF16 = jnp.bfloat16

GRID_W = 64
SCAN_BLOCK = 128
S5_GROUP = 16
S5_SUB = 16
ML_HEADS = 4
N_MIXERS = 3
LN_EPS = 1e-5
VMEM_LIMIT = 56 * 1024 * 1024


def _cparams(*sem):
    return pltpu.CompilerParams(dimension_semantics=sem, vmem_limit_bytes=VMEM_LIMIT)


def _const_spec(shape):
    nd = len(shape)
    return pl.BlockSpec(shape, lambda *_: (0,) * nd, pipeline_mode=pl.Buffered(1))


def _ln(x, g, b):
    mu = jnp.mean(x, axis=-1, keepdims=True)
    xc = x - mu
    var = jnp.mean(xc * xc, axis=-1, keepdims=True)
    return xc * lax.rsqrt(var + LN_EPS) * g + b


def _sigmoid(x):
    return 1.0 / (1.0 + jnp.exp(-x))


def _silu(x):
    return x * _sigmoid(x)


def _gelu(x):
    return 0.5 * x * (1.0 + jnp.tanh(math.sqrt(2.0 / math.pi) * (x + 0.044715 * (x * x * x))))


def _bdot(a, b):
    return jnp.dot(a.astype(BF16), b.astype(BF16), preferred_element_type=F32)


def _mod_kernel(c_ref, w_ref, b_ref, o_ref):
    o_ref[0] = _bdot(_silu(c_ref[...]), w_ref[0]) + b_ref[0]


def _modulation(c, c_ctx, mod_w, mod_b):
    depth, d, d6 = mod_w.shape
    bsz = c.shape[0]
    rows = 8
    cs = jnp.zeros((rows, d), F32).at[:bsz].set(c).at[bsz].set(c_ctx)
    tn = d6 // 6
    return pl.pallas_call(
        _mod_kernel,
        grid=(depth, d6 // tn),
        in_specs=[pl.BlockSpec((rows, d), lambda i, j: (0, 0)),
                  pl.BlockSpec((1, d, tn), lambda i, j: (i, 0, j)),
                  pl.BlockSpec((1, 1, tn), lambda i, j: (i, 0, j))],
        out_specs=pl.BlockSpec((1, rows, tn), lambda i, j: (i, 0, j)),
        out_shape=jax.ShapeDtypeStruct((depth, rows, d6), F32),
        compiler_params=_cparams("parallel", "parallel"),
        name="modulation",
    )(cs, mod_w, mod_b.reshape(depth, 1, d6))


def _ffn_kernel(*refs, width, tm, n_chunks, alpha, halo):
    if halo:
        top_ref, x_ref, bot_ref = refs[:3]
        refs = refs[3:]
    else:
        x_ref = refs[0]
        refs = refs[1:]
    mod_ref, wg_ref, wu_ref, cw_ref, cb_ref, wd_ref, lng_ref, lnb_ref, o_ref, xbuf, gbuf, acc = refs
    i = pl.program_id(1)
    nt = pl.num_programs(1)
    pad = 8
    hw = width if halo else 0
    fc = gbuf.shape[1]

    sh2 = mod_ref[0, 3:4, :]
    sc2 = 1.0 + mod_ref[0, 4:5, :]
    g2 = 1.0 + mod_ref[0, 5:6, :]
    x = x_ref[0]
    xbuf[hw:hw + tm, :] = (x * sc2 + sh2).astype(BF16)
    if halo:
        xbuf[0:hw, :] = (top_ref[0] * sc2 + sh2).astype(BF16)
        xbuf[hw + tm:, :] = (bot_ref[0] * sc2 + sh2).astype(BF16)
    acc[...] = jnp.zeros_like(acc)
    gbuf[0:pad, :] = jnp.zeros((pad, fc), F32)
    gbuf[pad + tm + 2 * hw:, :] = jnp.zeros((pad, fc), F32)

    col = lax.broadcasted_iota(jnp.int32, (tm, fc), 0) & (width - 1)
    not_first = col > 0
    not_last = col < width - 1

    def chunk(j, carry):
        gbuf[pad:pad + tm + 2 * hw, :] = jnp.dot(xbuf[...], wg_ref[j], preferred_element_type=F32)
        if halo:
            @pl.when(i == 0)
            def _():
                gbuf[pad:pad + hw, :] = jnp.zeros((hw, fc), F32)

            @pl.when(i == nt - 1)
            def _():
                gbuf[pad + hw + tm:pad + 2 * hw + tm, :] = jnp.zeros((hw, fc), F32)

        def taps(dc):
            t = None
            for dr in ((-1, 0, 1) if halo else (0,)):
                start = pad + hw + dr * width + dc
                w = cw_ref[j, 3 * (dr + 1) + dc + 1:3 * (dr + 1) + dc + 2, :]
                term = gbuf[start:start + tm, :] * w
                t = term if t is None else t + term
            return t

        gate = (taps(0) + jnp.where(not_first, taps(-1), 0.0) + jnp.where(not_last, taps(1), 0.0)
                + cb_ref[j])
        up = jnp.dot(xbuf[hw:hw + tm, :], wu_ref[j], preferred_element_type=F32)
        act = (_gelu(gate) * up).astype(BF16)
        acc[...] += jnp.dot(act, wd_ref[j], preferred_element_type=F32)
        return carry

    lax.fori_loop(0, n_chunks, chunk, 0)
    o_ref[0] = _ln(alpha * x + g2 * acc[...], lng_ref[...], lnb_ref[...])


def _ffn_chunk(f):
    for fc in (256, 128):
        if f % fc == 0:
            return fc
    raise ValueError(f"ffn hidden size {f} is not a multiple of 128")


def _conv_ffn_ln(h, mod, w_gate, w_up, conv_w, conv_b, w_down, ln_g, ln_b, *, width, alpha, mod_row=None):
    bsz, length, d = h.shape
    f = w_gate.shape[1]
    fc = _ffn_chunk(f)
    nf = f // fc
    rows = length // width
    halo = rows > 1
    tm = min(length, 512) if halo else length
    assert length % tm == 0 and tm % width == 0 and width & (width - 1) == 0
    nt = length // tm
    r = tm // width
    wg = w_gate.astype(BF16).reshape(d, nf, fc).transpose(1, 0, 2)
    wu = w_up.astype(BF16).reshape(d, nf, fc).transpose(1, 0, 2)
    wd = w_down.astype(BF16).reshape(nf, fc, d)
    cw = conv_w.reshape(9, nf, fc).transpose(1, 0, 2)
    cb = conv_b.reshape(nf, 1, fc)
    mod_map = (lambda b, i: (b, 0, 0)) if mod_row is None else (lambda b, i: (mod_row, 0, 0))
    x_spec = pl.BlockSpec((1, tm, d), lambda b, i: (b, i, 0))
    in_specs, args = [x_spec], [h]
    if halo:
        nrow = length // width
        in_specs = [pl.BlockSpec((1, width, d), lambda b, i: (b, jnp.maximum(i * r - 1, 0), 0)), x_spec,
                    pl.BlockSpec((1, width, d), lambda b, i: (b, jnp.minimum((i + 1) * r, nrow - 1), 0))]
        args = [h, h, h]
    in_specs += [pl.BlockSpec((1, 8, d), mod_map), _const_spec(wg.shape), _const_spec(wu.shape),
                 _const_spec(cw.shape), _const_spec(cb.shape), _const_spec(wd.shape),
                 _const_spec((1, d)), _const_spec((1, d))]
    args += [mod, wg, wu, cw, cb, wd, ln_g.reshape(1, d), ln_b.reshape(1, d)]
    hw = width if halo else 0
    return pl.pallas_call(
        functools.partial(_ffn_kernel, width=width, tm=tm, n_chunks=nf, alpha=alpha, halo=halo),
        grid=(bsz, nt),
        in_specs=in_specs,
        out_specs=pl.BlockSpec((1, tm, d), lambda b, i: (b, i, 0)),
        out_shape=jax.ShapeDtypeStruct((bsz, length, d), F32),
        scratch_shapes=[pltpu.VMEM((tm + 2 * hw, d), BF16),
                        pltpu.VMEM((tm + 2 * hw + 16, fc), F32),
                        pltpu.VMEM((tm, d), F32)],
        compiler_params=_cparams("parallel", "parallel"),
        name="conv_ffn_ln",
    )(*args)


def _s5_operators(lam_re, lam_im, log_dt, b_re, b_im, c_re, c_im):
    hi = lax.Precision.HIGHEST
    s = S5_SUB
    e_cols, f_rows, m_tot, a_rows = [], [], None, []
    for direction in range(2):
        lre = jnp.minimum(lam_re[direction], -1e-4)
        lim = lam_im[direction]
        dt = jnp.exp(log_dt[direction])[:, None]
        mag = jnp.exp(lre * dt)
        ar, ai = mag * jnp.cos(lim * dt), mag * jnp.sin(lim * dt)
        nr, ni = ar - 1.0, ai
        den = lre * lre + lim * lim
        cr = (nr * lre + ni * lim) / den
        ci = (ni * lre - nr * lim) / den
        bbr = cr[..., None] * b_re[direction] - ci[..., None] * b_im[direction]
        bbi = cr[..., None] * b_im[direction] + ci[..., None] * b_re[direction]
        pr, pi = [jnp.ones_like(ar)], [jnp.zeros_like(ai)]
        for _ in range(s):
            pr.append(pr[-1] * ar - pi[-1] * ai)
            pi.append(pr[-2] * ai + pi[-1] * ar)
        pr, pi = jnp.stack(pr), jnp.stack(pi)
        abr = pr[:s, :, :, None] * bbr - pi[:s, :, :, None] * bbi
        abi = pr[:s, :, :, None] * bbi + pi[:s, :, :, None] * bbr
        ccr, cci = c_re[direction], c_im[direction]
        kern = (jnp.einsum('gop,tgpi->tgoi', ccr, abr, precision=hi)
                - jnp.einsum('gop,tgpi->tgoi', cci, abi, precision=hi))
        j = jnp.arange(s)[:, None]
        t = jnp.arange(s)[None, :]
        lag = (t - j) if direction == 0 else (j - t)
        m = jnp.where((lag >= 0)[:, :, None, None, None], kern[jnp.clip(lag, 0, s - 1)], 0.0)
        m = m.transpose(2, 0, 4, 1, 3)
        m_tot = m if m_tot is None else m_tot + m
        order = (s - 1 - jnp.arange(s)) if direction == 0 else jnp.arange(s)
        er = abr[order].transpose(1, 0, 3, 2)
        ei = abi[order].transpose(1, 0, 3, 2)
        e_cols.append((er, ei))
        expo = (jnp.arange(s) + 1) if direction == 0 else (s - jnp.arange(s))
        zr, zi = pr[expo], pi[expo]
        f_re = jnp.einsum('gop,tgp->gpto', ccr, zr) - jnp.einsum('gop,tgp->gpto', cci, zi)
        f_im = -(jnp.einsum('gop,tgp->gpto', ccr, zi) + jnp.einsum('gop,tgp->gpto', cci, zr))
        f_rows.append((f_re, f_im))
        a_rows.append((jnp.concatenate([pr[s], pr[s]], -1), jnp.concatenate([-pi[s], pi[s]], -1)))
    g = lam_re.shape[1]
    w = s * S5_GROUP
    (fer, fei), (ber, bei) = e_cols
    e_mat = jnp.concatenate([fer, fei, ber, bei, fei, fer, bei, ber], axis=-1).reshape(g, w, -1)
    (ffr, ffi), (bfr, bfi) = f_rows
    f_mat = jnp.concatenate([ffr, ffi, bfr, bfi], axis=1).reshape(g, -1, w)
    a_mat = jnp.stack([a_rows[0][0], a_rows[0][1], a_rows[1][0], a_rows[1][1]], axis=1)
    return e_mat, m_tot.reshape(g, w, w), f_mat, a_mat


def _pick_group(res, n, gp, bsz):
    if gp == 1:
        return res
    sel = (lax.broadcasted_iota(jnp.int32, (res.shape[0], n), 0) & 7) // bsz
    out = res[:, :n]
    for k in range(1, gp):
        out = jnp.where(sel == k, res[:, k * n:(k + 1) * n], out)
    return out


def _s5_kernel(xc_ref, xl_ref, mc_ref, ml_ref, d_ref, a_ref, e_ref, m_ref, f_ref, yc_ref, yl_ref, se,
               *, bsz, gp, rc):
    rows_c, w = xc_ref.shape[1], xc_ref.shape[2]
    rows_l = xl_ref.shape[1]
    p2 = a_ref.shape[-1]

    def modulated(x_ref, mod_ref, r0, n):
        x = x_ref[0, pl.ds(r0, n), :].reshape(n // 8, 8, w)
        return (x * (1.0 + mod_ref[0, 1]) + mod_ref[0, 0]).reshape(n, w)

    def chunks(rows):
        return [(r0, min(rc, rows - r0)) for r0 in range(0, rows, rc)]

    segments = ((xc_ref, mc_ref, yc_ref, 0, rows_c), (xl_ref, ml_ref, yl_ref, rows_c, rows_l))

    for x_ref, mod_ref, _, base, rows in segments:
        for r0, n in chunks(rows):
            u = modulated(x_ref, mod_ref, r0, n).astype(BF16)
            e = jnp.dot(u, e_ref[0], preferred_element_type=F32)
            se[base + r0:base + r0 + n, :] = _pick_group(e, 4 * p2, gp, bsz)

    a1f, a2f, a1b, a2b = a_ref[0, 0], a_ref[0, 1], a_ref[0, 2], a_ref[0, 3]

    def make_step(base, tiles):
        def step(s, carry):
            vf, wf, vb, wb = carry
            rf = pl.multiple_of(base + s * 8, 8)
            rb = pl.multiple_of(base + (tiles - 1 - s) * 8, 8)
            ef_v = se[pl.ds(rf, 8), 0:p2]
            ef_w = se[pl.ds(rf, 8), 2 * p2:3 * p2]
            eb_v = se[pl.ds(rb, 8), p2:2 * p2]
            eb_w = se[pl.ds(rb, 8), 3 * p2:4 * p2]
            se[pl.ds(rf, 8), 0:p2] = vf
            se[pl.ds(rb, 8), p2:2 * p2] = vb
            return (a1f * vf + a2f * wf + ef_v, a1f * wf - a2f * vf + ef_w,
                    a1b * vb + a2b * wb + eb_v, a1b * wb - a2b * vb + eb_w)
        return step

    zero = jnp.zeros((8, p2), F32)
    carry = (zero, zero, zero, zero)
    carry = lax.fori_loop(0, rows_c // 8, make_step(0, rows_c // 8), carry, unroll=2)
    lax.fori_loop(0, rows_l // 8, make_step(rows_c, rows_l // 8), carry, unroll=2)

    for x_ref, mod_ref, y_ref, base, rows in segments:
        for r0, n in chunks(rows):
            u = modulated(x_ref, mod_ref, r0, n)
            s_in = se[base + r0:base + r0 + n, 0:2 * p2].astype(BF16)
            y = (_pick_group(jnp.dot(u.astype(BF16), m_ref[0], preferred_element_type=F32), w, gp, bsz)
                 + _pick_group(jnp.dot(s_in, f_ref[0], preferred_element_type=F32), w, gp, bsz))
            y = y + (u.reshape(n // 8, 8, w) * d_ref[0]).reshape(n, w)
            y_ref[0, pl.ds(r0, n), :] = _gelu(y).astype(BF16)


def _s5_rows(h, gp, col_major_rows=None):
    bsz, length, d = h.shape
    gt = d // S5_GROUP // gp
    if col_major_rows is None:
        x = h.reshape(bsz, length // S5_SUB, S5_SUB, gt, gp, S5_GROUP).transpose(3, 1, 4, 0, 2, 5)
    else:
        rows, cols = col_major_rows, length // col_major_rows
        x = h.reshape(bsz, rows // S5_SUB, S5_SUB, cols, gt, gp, S5_GROUP).transpose(4, 3, 1, 5, 0, 2, 6)
    return x.reshape(gt, (length // S5_SUB) * gp * bsz, S5_SUB * S5_GROUP)


def _s5_unrows(y, bsz, length, gp, col_major_rows=None):
    gt = y.shape[0]
    d = gt * gp * S5_GROUP
    if col_major_rows is None:
        x = y.reshape(gt, length // S5_SUB, gp, bsz, S5_SUB, S5_GROUP).transpose(3, 1, 4, 0, 2, 5)
    else:
        rows, cols = col_major_rows, length // col_major_rows
        x = y.reshape(gt, cols, rows // S5_SUB, gp, bsz, S5_SUB, S5_GROUP).transpose(4, 2, 5, 1, 0, 3, 6)
    return x.reshape(bsz, length, d)


def _s5_pattern(v, gp, reps):
    d = v.shape[-1]
    gt = d // S5_GROUP // gp
    v = jnp.broadcast_to(v.reshape(-1, d), (reps, d))
    x = v.reshape(reps, gt, gp, 1, S5_GROUP).transpose(1, 2, 0, 3, 4)
    x = jnp.broadcast_to(x, (gt, gp, reps, S5_SUB, S5_GROUP))
    return x.reshape(gt, gp * reps, S5_SUB * S5_GROUP)


def _s5_core(h_lat, h_ctx, mod_l, mod_c, params, col_major):
    lam_re, lam_im, log_dt, b_re, b_im, c_re, c_im, d_skip = params
    bsz, length, d = h_lat.shape
    ctx_len = h_ctx.shape[1]
    assert 8 % bsz == 0 and (d // S5_GROUP) % (8 // bsz) == 0
    gp = 8 // bsz
    gt = d // S5_GROUP // gp
    w = S5_SUB * S5_GROUP
    rows_cm = (length // GRID_W) if col_major else None
    xl = _s5_rows(h_lat, gp, rows_cm)
    xc = _s5_rows(h_ctx, gp)
    e_mat, m_mat, f_mat, a_mat = _s5_operators(lam_re, lam_im, log_dt, b_re, b_im, c_re, c_im)
    p2 = a_mat.shape[-1]
    cat = lambda m: m.reshape(gt, gp, m.shape[1], m.shape[2]).transpose(0, 2, 1, 3).reshape(gt, m.shape[1], -1)
    e_cat, m_cat, f_cat = (cat(m).astype(BF16) for m in (e_mat, m_mat, f_mat))
    a_pat = jnp.broadcast_to(a_mat.reshape(gt, gp, 1, 4, p2), (gt, gp, bsz, 4, p2)).transpose(0, 3, 1, 2, 4)
    a_pat = a_pat.reshape(gt, 4, 8, p2)
    ml_pat = jnp.stack([_s5_pattern(mod_l[:, 0], gp, bsz), _s5_pattern(mod_l[:, 1], gp, bsz)], axis=1)
    mc_pat = jnp.stack([_s5_pattern(mod_c[:, 0], gp, bsz), _s5_pattern(mod_c[:, 1], gp, bsz)], axis=1)
    d_pat = _s5_pattern(d_skip, gp, bsz)
    rows_l, rows_c = xl.shape[1], xc.shape[1]
    rc = 512
    tile = lambda *shape: pl.BlockSpec((1,) + shape, lambda g: (g,) + (0,) * len(shape))
    yc, yl = pl.pallas_call(
        functools.partial(_s5_kernel, bsz=bsz, gp=gp, rc=rc),
        grid=(gt,),
        in_specs=[tile(rows_c, w), tile(rows_l, w), tile(2, 8, w), tile(2, 8, w), tile(8, w), tile(4, 8, p2),
                  tile(w, gp * 4 * p2), tile(w, gp * w), tile(2 * p2, gp * w)],
        out_specs=[tile(rows_c, w), tile(rows_l, w)],
        out_shape=[jax.ShapeDtypeStruct((gt, rows_c, w), BF16), jax.ShapeDtypeStruct((gt, rows_l, w), BF16)],
        scratch_shapes=[pltpu.VMEM((rows_c + rows_l, 4 * p2), F32)],
        compiler_params=_cparams("parallel"),
        name="s5_scan",
    )(xc, xl, mc_pat, ml_pat, d_pat, a_pat, e_cat, m_cat, f_cat)
    return _s5_unrows(yl, bsz, length, gp, rows_cm), _s5_unrows(yc, bsz, ctx_len, gp)


def _glu_ln_kernel(y_ref, h_ref, mod_ref, w_ref, b_ref, lng_ref, lnb_ref, o_ref, *, alpha):
    d = h_ref.shape[-1]
    z = jnp.dot(y_ref[0], w_ref[...], preferred_element_type=F32) + b_ref[...]
    glu = z[:, :d] * _sigmoid(z[:, d:])
    g1 = 1.0 + mod_ref[0, 2:3, :]
    o_ref[0] = _ln(alpha * h_ref[0] + g1 * glu, lng_ref[...], lnb_ref[...])


def _glu_ln(y, h, mod, w_glu, b_glu, ln_g, ln_b, *, alpha, mod_row=None):
    bsz, length, d = h.shape
    tm = min(length, 512)
    mod_map = (lambda b, i: (b, 0, 0)) if mod_row is None else (lambda b, i: (mod_row, 0, 0))
    tok = lambda: pl.BlockSpec((1, tm, d), lambda b, i: (b, i, 0))
    return pl.pallas_call(
        functools.partial(_glu_ln_kernel, alpha=alpha),
        grid=(bsz, length // tm),
        in_specs=[tok(), tok(), pl.BlockSpec((1, 8, d), mod_map), _const_spec((d, 2 * d)), _const_spec((1, 2 * d)),
                  _const_spec((1, d)), _const_spec((1, d))],
        out_specs=tok(),
        out_shape=jax.ShapeDtypeStruct((bsz, length, d), F32),
        compiler_params=_cparams("parallel", "parallel"),
        name="glu_ln",
    )(y, h, mod, w_glu.astype(BF16), b_glu.reshape(1, 2 * d), ln_g.reshape(1, d), ln_b.reshape(1, d))


def _mod_map(mod_row):
    return (lambda b, i: (b, 0, 0)) if mod_row is None else (lambda b, i: (mod_row, 0, 0))


def _ml_up_kernel(h_ref, mod_ref, w_ref, xm_ref, z_ref):
    e = xm_ref.shape[-1]
    u = (h_ref[0] * (1.0 + mod_ref[0, 1:2, :]) + mod_ref[0, 0:1, :]).astype(BF16)
    y = jnp.dot(u, w_ref[...], preferred_element_type=F32)
    xm_ref[0] = y[:, :e]
    z_ref[0] = y[:, e:].astype(BF16)


def _ml_up(h, mod, w_up, *, mod_row=None):
    bsz, length, d = h.shape
    e = w_up.shape[1] // 2
    tm = min(length, 512)
    tok = lambda n: pl.BlockSpec((1, tm, n), lambda b, i: (b, i, 0))
    return pl.pallas_call(
        _ml_up_kernel,
        grid=(bsz, length // tm),
        in_specs=[tok(d), pl.BlockSpec((1, 8, d), _mod_map(mod_row)), _const_spec((d, 2 * e))],
        out_specs=[tok(e), tok(e)],
        out_shape=[jax.ShapeDtypeStruct((bsz, length, e), F32), jax.ShapeDtypeStruct((bsz, length, e), BF16)],
        compiler_params=_cparams("parallel", "parallel"),
        name="ml_up",
    )(h, mod, w_up.astype(BF16))


def _ml_qkv_kernel(prev_ref, x_ref, next_ref, cw_ref, cb_ref, w_ref, wg_ref, bg_ref,
                   xc_ref, q_ref, k_ref, v_ref, g_ref, xbuf, *, k_scale):
    i = pl.program_id(1)
    nt = pl.num_programs(1)
    tm, e = x_ref.shape[1], x_ref.shape[2]
    hp = prev_ref.shape[1]
    xbuf[0:hp, :] = jnp.where(i > 0, prev_ref[0], 0.0)
    xbuf[hp:hp + tm, :] = x_ref[0]
    xbuf[hp + tm:, :] = jnp.where(i < nt - 1, next_ref[0], 0.0)
    x = x_ref[0]
    conv = (cw_ref[0:1, :] * xbuf[hp - 1:hp - 1 + tm, :] + cw_ref[1:2, :] * x
            + cw_ref[2:3, :] * xbuf[hp + 1:hp + 1 + tm, :] + cb_ref[...])
    xc = _silu(conv).astype(BF16)
    xc_ref[0] = xc
    q = jnp.dot(xc, w_ref[0], preferred_element_type=F32).astype(BF16)
    k = (jnp.dot(xc, w_ref[1], preferred_element_type=F32) * k_scale).astype(BF16)
    v = jnp.dot(x.astype(BF16), w_ref[2], preferred_element_type=F32).astype(BF16)
    q_ref[0] = q
    k_ref[0] = k
    v_ref[0] = v
    g_ref[0] = (jnp.dot(q, wg_ref[0], preferred_element_type=F32) + jnp.dot(k, wg_ref[1], preferred_element_type=F32)
                + jnp.dot(v, wg_ref[2], preferred_element_type=F32) + bg_ref[...])


def _ml_qkv(xm, conv_w, conv_b, w_q, w_k, w_v, w_gates, b_gates):
    bsz, length, e = xm.shape
    tm = min(length, 256)
    hp = 8
    r = tm // hp
    nb = length // hp
    ng = w_gates.shape[-1]
    lanes = 128
    w3 = jnp.stack([w_q, w_k, w_v]).astype(BF16)
    wg = jnp.zeros((3, e, lanes), BF16).at[:, :, :ng].set(w_gates.astype(BF16))
    bg = jnp.zeros((1, lanes), F32).at[0, :ng].set(b_gates)
    cw = jnp.zeros((8, e), F32).at[:conv_w.shape[0]].set(conv_w)
    tok = lambda n: pl.BlockSpec((1, tm, n), lambda b, i: (b, i, 0))
    return pl.pallas_call(
        functools.partial(_ml_qkv_kernel, k_scale=(e // ML_HEADS) ** -0.5),
        grid=(bsz, length // tm),
        in_specs=[pl.BlockSpec((1, hp, e), lambda b, i: (b, jnp.maximum(i * r - 1, 0), 0)), tok(e),
                  pl.BlockSpec((1, hp, e), lambda b, i: (b, jnp.minimum((i + 1) * r, nb - 1), 0)),
                  _const_spec((8, e)), _const_spec((1, e)), _const_spec((3, e, e)), _const_spec((3, e, lanes)),
                  _const_spec((1, lanes))],
        out_specs=[tok(e), tok(e), tok(e), tok(e), tok(lanes)],
        out_shape=[jax.ShapeDtypeStruct((bsz, length, e), BF16)] * 4
                  + [jax.ShapeDtypeStruct((bsz, length, lanes), F32)],
        scratch_shapes=[pltpu.VMEM((tm + 2 * hp, e), F32)],
        compiler_params=_cparams("parallel", "parallel"),
        name="ml_qkv",
    )(xm, xm, xm, cw, conv_b.reshape(1, e), w3, wg, bg)


def _log_sigmoid(x):
    return jnp.minimum(x, 0.0) - jnp.log(1.0 + jnp.exp(-jnp.abs(x)))


def _ml_scan_kernel(*refs, nblk_c, heads):
    (qcf, kcf, vcf, gccf, grcf, qcb, kcb, vcb, gccb, grcb,
     qlf, klf, vlf, gclf, grlf, qlb, klb, vlb, gclb, grlb,
     hcf_ref, hcb_ref, hlf_ref, hlb_ref, c_sc, n_sc, m_sc) = refs
    i = pl.program_id(1)
    is_ctx = i < nblk_c
    t = qcf.shape[1]
    e = qcf.shape[2]
    dh = e // heads

    @pl.when(i == 0)
    def _():
        c_sc[...] = jnp.zeros_like(c_sc)
        n_sc[...] = jnp.zeros_like(n_sc)
        m_sc[...] = jnp.zeros_like(m_sc)

    row = lax.broadcasted_iota(jnp.int32, (t, t), 0)
    col = lax.broadcasted_iota(jnp.int32, (t, t), 1)
    pick = lambda c_ref, l_ref: jnp.where(is_ctx, c_ref[0], l_ref[0])
    outs = []
    for direction, blk in enumerate(((qcf, kcf, vcf, gccf, grcf, qlf, klf, vlf, gclf, grlf),
                                     (qcb, kcb, vcb, gccb, grcb, qlb, klb, vlb, gclb, grlb))):
        q_all, k_all, v_all, gc, gr = (pick(blk[j], blk[j + 5]) for j in range(5))
        mask = (col <= row) if direction == 0 else (col >= row)
        mask_t = (row <= col) if direction == 0 else (row >= col)
        h_heads = []
        for head in range(heads):
            r = direction * heads + head
            ci = direction * 2 * heads + head
            cf = ci + heads
            q = q_all[:, head * dh:(head + 1) * dh]
            k = k_all[:, head * dh:(head + 1) * dh]
            v = v_all[:, head * dh:(head + 1) * dh]
            ig_col, ig_row = gc[:, ci:ci + 1], gr[ci:ci + 1, :]
            lf_col, lf_row = _log_sigmoid(gc[:, cf:cf + 1]), _log_sigmoid(gr[cf:cf + 1, :])
            m_prev = m_sc[r, :, 0:1]
            b_col = jnp.sum(jnp.where(mask, jnp.broadcast_to(lf_row, (t, t)), 0.0), axis=1, keepdims=True)
            b_row = jnp.sum(jnp.where(mask_t, jnp.broadcast_to(lf_col, (t, t)), 0.0), axis=0, keepdims=True)
            b_tot = jnp.sum(lf_row, axis=1, keepdims=True)
            logd = jnp.where(mask, b_col - b_row + ig_row, -jnp.inf)
            m_inter = b_col + m_prev
            m_t = jnp.maximum(m_inter, jnp.max(logd, axis=1, keepdims=True))
            w_inter = jnp.exp(m_inter - m_t)
            qk = lax.dot_general(q, k, (((1,), (1,)), ((), ())), preferred_element_type=F32)
            s = jnp.exp(logd - m_t) * qk
            c_t = c_sc[r]
            n_row = n_sc[r]
            num = (w_inter * jnp.dot(q, c_t.astype(BF16), preferred_element_type=F32)
                   + jnp.dot(s.astype(BF16), v, preferred_element_type=F32))
            den = (w_inter * jnp.sum(q.astype(F32) * n_row, axis=1, keepdims=True)
                   + jnp.sum(s, axis=1, keepdims=True))
            h_heads.append(num / jnp.maximum(jnp.abs(den), jnp.exp(-m_t)))
            decay = b_tot - b_col + ig_col
            m_new = jnp.maximum(b_tot + m_prev, jnp.max(decay, axis=0, keepdims=True))
            w_prev = jnp.exp(b_tot + m_prev - m_new)
            w_r = jnp.exp(decay - m_new)
            wv = (w_r * v.astype(F32)).astype(BF16)
            c_sc[r] = w_prev * c_t + lax.dot_general(k, wv, (((0,), (0,)), ((), ())), preferred_element_type=F32)
            n_sc[r] = w_prev * n_row + jnp.sum(w_r * k.astype(F32), axis=0, keepdims=True)
            m_sc[r] = jnp.broadcast_to(m_new, m_sc.shape[1:])
        outs.append(jnp.concatenate(h_heads, axis=1))

    @pl.when(is_ctx)
    def _():
        hcf_ref[0] = outs[0]
        hcb_ref[0] = outs[1]

    @pl.when(jnp.logical_not(is_ctx))
    def _():
        hlf_ref[0] = outs[0]
        hlb_ref[0] = outs[1]


def _ml_scan(qkv_c, g_c, qkv_l, g_l):
    bsz, len_c, e = qkv_c[0].shape
    len_l = qkv_l[0].shape[1]
    t = SCAN_BLOCK
    nc, nl = len_c // t, len_l // t
    lanes = g_c.shape[-1]
    rows = 16
    grow = lambda g: jnp.swapaxes(g[:, :, :rows], 1, 2)
    cf = lambda i: jnp.minimum(i, nc - 1)
    cb = lambda i: jnp.maximum(nc - 1 - i, 0)
    lf = lambda i: jnp.maximum(i - nc, 0)
    lb = lambda i: jnp.minimum(nl - 1 - (i - nc), nl - 1)
    def specs(idx):
        tok = pl.BlockSpec((1, t, e), lambda b, i: (b, idx(i), 0))
        return [tok, tok, tok, pl.BlockSpec((1, t, lanes), lambda b, i: (b, idx(i), 0)),
                pl.BlockSpec((1, rows, t), lambda b, i: (b, 0, idx(i)))]
    args_c = list(qkv_c) + [g_c, grow(g_c)]
    args_l = list(qkv_l) + [g_l, grow(g_l)]
    out = lambda idx: pl.BlockSpec((1, t, e), lambda b, i: (b, idx(i), 0))
    heads = ML_HEADS
    dh = e // heads
    return pl.pallas_call(
        functools.partial(_ml_scan_kernel, nblk_c=nc, heads=heads),
        grid=(bsz, nc + nl),
        in_specs=specs(cf) + specs(cb) + specs(lf) + specs(lb),
        out_specs=[out(cf), out(cb), out(lf), out(lb)],
        out_shape=[jax.ShapeDtypeStruct((bsz, len_c, e), F32)] * 2 + [jax.ShapeDtypeStruct((bsz, len_l, e), F32)] * 2,
        scratch_shapes=[pltpu.VMEM((2 * heads, dh, dh), F32), pltpu.VMEM((2 * heads, 1, dh), F32),
                        pltpu.VMEM((2 * heads, 1, 128), F32)],
        compiler_params=_cparams("parallel", "arbitrary"),
        name="ml_scan",
    )(*(args_c + args_c + args_l + args_l))


def _ml_out_kernel(hf_ref, hb_ref, xc_ref, z_ref, h_ref, mod_ref, gn_ref, sk_ref, w_ref, lng_ref, lnb_ref, o_ref,
                   *, alpha, heads):
    e = hf_ref.shape[-1]
    dh = e // heads
    hh = hf_ref[0] + hb_ref[0]
    parts = []
    for head in range(heads):
        x = hh[:, head * dh:(head + 1) * dh]
        mu = jnp.mean(x, axis=-1, keepdims=True)
        xc = x - mu
        var = jnp.mean(xc * xc, axis=-1, keepdims=True)
        parts.append(xc * lax.rsqrt(var + LN_EPS))
    hn = jnp.concatenate(parts, axis=1) * gn_ref[...] + sk_ref[...] * xc_ref[0].astype(F32)
    a = (hn * _silu(z_ref[0].astype(F32))).astype(BF16)
    y = jnp.dot(a, w_ref[...], preferred_element_type=F32)
    g1 = 1.0 + mod_ref[0, 2:3, :]
    o_ref[0] = _ln(alpha * h_ref[0] + g1 * y, lng_ref[...], lnb_ref[...])


def _ml_out(hf, hb, xc, z, h, mod, gn_g, skip, w_down, ln_g, ln_b, *, alpha, mod_row=None):
    bsz, length, d = h.shape
    e = hf.shape[-1]
    tm = min(length, 256)
    tok = lambda n: pl.BlockSpec((1, tm, n), lambda b, i: (b, i, 0))
    return pl.pallas_call(
        functools.partial(_ml_out_kernel, alpha=alpha, heads=ML_HEADS),
        grid=(bsz, length // tm),
        in_specs=[tok(e), tok(e), tok(e), tok(e), tok(d), pl.BlockSpec((1, 8, d), _mod_map(mod_row)),
                  _const_spec((1, e)), _const_spec((1, e)), _const_spec((e, d)), _const_spec((1, d)),
                  _const_spec((1, d))],
        out_specs=tok(d),
        out_shape=jax.ShapeDtypeStruct((bsz, length, d), F32),
        compiler_params=_cparams("parallel", "parallel"),
        name="ml_out",
    )(hf, hb, xc, z, h, mod, gn_g.reshape(1, e), skip.reshape(1, e), w_down.astype(BF16),
      ln_g.reshape(1, d), ln_b.reshape(1, d))


def _mlstm_layer(h_lat, h_ctx, mod_l, mod_c, p, ln_g, ln_b, *, alpha, need_ctx):
    (w_up, conv_w, conv_b, w_q, w_k, w_v, w_gates, b_gates, gn_g, skip, w_down) = p
    xm_l, z_l = _ml_up(h_lat, mod_l, w_up)
    xm_c, z_c = _ml_up(h_ctx, mod_c, w_up, mod_row=0)
    xc_l, q_l, k_l, v_l, g_l = _ml_qkv(xm_l, conv_w, conv_b, w_q, w_k, w_v, w_gates, b_gates)
    xc_c, q_c, k_c, v_c, g_c = _ml_qkv(xm_c, conv_w, conv_b, w_q, w_k, w_v, w_gates, b_gates)
    hcf, hcb, hlf, hlb = _ml_scan((q_c, k_c, v_c), g_c, (q_l, k_l, v_l), g_l)
    out_l = _ml_out(hlf, hlb, xc_l, z_l, h_lat, mod_l, gn_g, skip, w_down, ln_g, ln_b, alpha=alpha)
    out_c = None
    if need_ctx:
        out_c = _ml_out(hcf, hcb, xc_c, z_c, h_ctx, mod_c, gn_g, skip, w_down, ln_g, ln_b, alpha=alpha, mod_row=0)
    return out_l, out_c


def _cv_in_kernel(h_ref, mod_ref, w_ref, b_ref, o_ref):
    d = h_ref.shape[-1]
    u = (h_ref[0] * (1.0 + mod_ref[0, 1:2, :]) + mod_ref[0, 0:1, :]).astype(BF16)
    y = jnp.dot(u, w_ref[...], preferred_element_type=F32) + b_ref[...]
    o_ref[0] = y[:, :d] * _sigmoid(y[:, d:])


def _cv_in(h, mod, w_in, b_in, *, mod_row=None):
    bsz, length, d = h.shape
    tm = min(length, 512)
    tok = lambda: pl.BlockSpec((1, tm, d), lambda b, i: (b, i, 0))
    return pl.pallas_call(
        _cv_in_kernel,
        grid=(bsz, length // tm),
        in_specs=[tok(), pl.BlockSpec((1, 8, d), _mod_map(mod_row)), _const_spec((d, 2 * d)),
                  _const_spec((1, 2 * d))],
        out_specs=tok(),
        out_shape=jax.ShapeDtypeStruct((bsz, length, d), F32),
        compiler_params=_cparams("parallel", "parallel"),
        name="cv_in",
    )(h, mod, w_in.astype(BF16), b_in.reshape(1, 2 * d))


def _cv_out_kernel(prev_ref, x_ref, next_ref, h_ref, mod_ref, dw_ref, db_ref, cg_ref, cb_ref, w_ref, b_ref,
                   lng_ref, lnb_ref, o_ref, xbuf, *, alpha, taps):
    i = pl.program_id(1)
    nt = pl.num_programs(1)
    tm = x_ref.shape[1]
    hp = prev_ref.shape[1]
    xbuf[0:hp, :] = jnp.where(i > 0, prev_ref[0], 0.0)
    xbuf[hp:hp + tm, :] = x_ref[0]
    xbuf[hp + tm:, :] = jnp.where(i < nt - 1, next_ref[0], 0.0)
    left = (taps - 1) // 2
    acc = None
    for k in range(taps):
        start = hp + k - left
        term = dw_ref[k:k + 1, :] * xbuf[start:start + tm, :]
        acc = term if acc is None else acc + term
    c = _silu(_ln(acc + db_ref[...], cg_ref[...], cb_ref[...])).astype(BF16)
    y = jnp.dot(c, w_ref[...], preferred_element_type=F32) + b_ref[...]
    g1 = 1.0 + mod_ref[0, 2:3, :]
    o_ref[0] = _ln(alpha * h_ref[0] + g1 * y, lng_ref[...], lnb_ref[...])


def _cv_out(hmid, h, mod, dw_w, dw_b, cln_g, cln_b, w_out, b_out, ln_g, ln_b, *, alpha, mod_row=None):
    bsz, length, d = h.shape
    taps = dw_w.shape[0]
    tm = min(length, 256)
    hp = 16
    assert (taps - 1) // 2 <= hp and taps // 2 <= hp
    r = tm // hp
    nb = length // hp
    dw = jnp.zeros((32, d), F32).at[:taps].set(dw_w)
    tok = lambda: pl.BlockSpec((1, tm, d), lambda b, i: (b, i, 0))
    vec = lambda: _const_spec((1, d))
    return pl.pallas_call(
        functools.partial(_cv_out_kernel, alpha=alpha, taps=taps),
        grid=(bsz, length // tm),
        in_specs=[pl.BlockSpec((1, hp, d), lambda b, i: (b, jnp.maximum(i * r - 1, 0), 0)), tok(),
                  pl.BlockSpec((1, hp, d), lambda b, i: (b, jnp.minimum((i + 1) * r, nb - 1), 0)),
                  tok(), pl.BlockSpec((1, 8, d), _mod_map(mod_row)), _const_spec((32, d)), vec(), vec(), vec(),
                  _const_spec((d, d)), vec(), vec(), vec()],
        out_specs=tok(),
        out_shape=jax.ShapeDtypeStruct((bsz, length, d), F32),
        scratch_shapes=[pltpu.VMEM((tm + 2 * hp, d), F32)],
        compiler_params=_cparams("parallel", "parallel"),
        name="cv_out",
    )(hmid, hmid, hmid, h, mod, dw, dw_b.reshape(1, d), cln_g.reshape(1, d), cln_b.reshape(1, d),
      w_out.astype(BF16), b_out.reshape(1, d), ln_g.reshape(1, d), ln_b.reshape(1, d))


def _jln(x, g, b):
    mu = x.mean(-1, keepdims=True)
    var = jnp.square(x - mu).mean(-1, keepdims=True)
    return (x - mu) * lax.rsqrt(var + LN_EPS) * g + b


def _j_dwconv1d(x, w, b):
    k = w.shape[0]
    y = lax.conv_general_dilated(x, w[:, None, :], (1,), [((k - 1) // 2, k // 2)],
                                 dimension_numbers=('NWC', 'WIO', 'NWC'), feature_group_count=x.shape[-1])
    return y + b


def _j_s5_discretise(lam_re, lam_im, log_dt, b_re, b_im):
    lre = jnp.minimum(lam_re, -1e-4)
    lim = lam_im
    dt = jnp.exp(log_dt)[:, None]
    mag = jnp.exp(lre * dt)
    lb_re, lb_im = mag * jnp.cos(lim * dt), mag * jnp.sin(lim * dt)
    nr, ni = lb_re - 1.0, lb_im
    den = lre * lre + lim * lim
    cr = (nr * lre + ni * lim) / den
    ci = (ni * lre - nr * lim) / den
    bb_re = cr[..., None] * b_re - ci[..., None] * b_im
    bb_im = cr[..., None] * b_im + ci[..., None] * b_re
    return lb_re, lb_im, bb_re, bb_im


def _j_combine(e1, e2):
    a1r, a1i, b1r, b1i = e1
    a2r, a2i, b2r, b2i = e2
    return (a2r * a1r - a2i * a1i, a2r * a1i + a2i * a1r,
            a2r * b1r - a2i * b1i + b2r, a2r * b1i + a2i * b1r + b2i)


def _j_s5_scan(u, lb_re, lb_im, bb_re, bb_im, c_re, c_im, s0):
    bsz, length, d = u.shape
    groups = d // S5_GROUP
    nblk = length // 128
    ub = jnp.moveaxis(u.reshape(bsz, nblk, 128, groups, S5_GROUP), 1, 0)

    def step(carry, u_blk):
        sr, si = carry
        bu_re = jnp.einsum('btgs,gps->btgp', u_blk, bb_re)
        bu_im = jnp.einsum('btgs,gps->btgp', u_blk, bb_im)
        a_re = jnp.broadcast_to(lb_re, bu_re.shape)
        a_im = jnp.broadcast_to(lb_im, bu_im.shape)
        pa_re, pa_im, pb_re, pb_im = lax.associative_scan(_j_combine, (a_re, a_im, bu_re, bu_im), axis=1)
        st_re = pa_re * sr[:, None] - pa_im * si[:, None] + pb_re
        st_im = pa_re * si[:, None] + pa_im * sr[:, None] + pb_im
        y = jnp.einsum('gsp,btgp->btgs', c_re, st_re) - jnp.einsum('gsp,btgp->btgs', c_im, st_im)
        return (st_re[:, -1], st_im[:, -1]), y.reshape(bsz, 128, d)

    s_end, ys = lax.scan(step, s0, ub)
    return jnp.moveaxis(ys, 0, 1).reshape(bsz, length, d), s_end


def _j_s5_mixer(ul, uc, lam_re, lam_im, log_dt, b_re, b_im, c_re, c_im, d_skip, w_glu, b_glu):
    y_lat, y_ctx = d_skip * ul, d_skip * uc
    bsz = ul.shape[0]
    groups, p = lam_re.shape[1], lam_re.shape[2]
    for direction in range(2):
        lb_re, lb_im, bb_re, bb_im = _j_s5_discretise(lam_re[direction], lam_im[direction], log_dt[direction],
                                                      b_re[direction], b_im[direction])
        cr, ci = c_re[direction], c_im[direction]
        zero = jnp.zeros((bsz, groups, p), F32)
        rev = (lambda a: a[:, ::-1]) if direction == 1 else (lambda a: a)
        yc, s_ctx = _j_s5_scan(rev(uc), lb_re, lb_im, bb_re, bb_im, cr, ci, (zero, zero))
        yl, _ = _j_s5_scan(rev(ul), lb_re, lb_im, bb_re, bb_im, cr, ci, s_ctx)
        y_ctx = y_ctx + rev(yc)
        y_lat = y_lat + rev(yl)

    def glu(y):
        z = jax.nn.gelu(y) @ w_glu + b_glu
        val, gate = jnp.split(z, 2, axis=-1)
        return val * jax.nn.sigmoid(gate)

    return glu(y_lat), glu(y_ctx)


def _j_mlstm_scan(q, k, v, ig, lf, state):
    bsz, nh, length, dh = q.shape
    nblk = length // SCAN_BLOCK
    blocks = lambda a: jnp.moveaxis(a.reshape(bsz, nh, nblk, SCAN_BLOCK, *a.shape[3:]), 2, 0)
    tri = jnp.tril(jnp.ones((SCAN_BLOCK, SCAN_BLOCK), bool))

    def step(carry, blk):
        C, n, m = carry
        qb, kb, vb, ib, fb = blk
        b = jnp.cumsum(fb, axis=-1)
        logd = jnp.where(tri, b[..., :, None] - b[..., None, :] + ib[..., None, :], -jnp.inf)
        m_inter = b + m[..., None]
        m_t = jnp.maximum(m_inter, logd.max(-1))
        w_inter = jnp.exp(m_inter - m_t)
        s = jnp.exp(logd - m_t[..., None]) * jnp.einsum('bhtd,bhsd->bhts', qb, kb)
        num = w_inter[..., None] * jnp.einsum('bhed,bhtd->bhte', C, qb) + jnp.einsum('bhts,bhse->bhte', s, vb)
        den = w_inter * jnp.einsum('bhd,bhtd->bht', n, qb) + s.sum(-1)
        h = num / jnp.maximum(jnp.abs(den), jnp.exp(-m_t))[..., None]
        decay = b[..., -1:] - b + ib
        m_new = jnp.maximum(b[..., -1] + m, decay.max(-1))
        w_prev = jnp.exp(b[..., -1] + m - m_new)
        w_r = jnp.exp(decay - m_new[..., None])
        C = w_prev[..., None, None] * C + jnp.einsum('bhse,bhsd->bhed', w_r[..., None] * vb, kb)
        n = w_prev[..., None] * n + jnp.einsum('bhs,bhsd->bhd', w_r, kb)
        return (C, n, m_new), h

    state, hs = lax.scan(step, state, tuple(blocks(a) for a in (q, k, v, ig, lf)))
    return jnp.moveaxis(hs, 0, 2).reshape(bsz, nh, length, dh), state


def _j_mlstm_project(u, w_up, conv_w, conv_b, w_q, w_k, w_v, w_gates, b_gates):
    bsz, length, _ = u.shape
    e = w_q.shape[0]
    dh = e // ML_HEADS
    xm, z = jnp.split(u @ w_up, 2, axis=-1)
    xc = jax.nn.silu(_j_dwconv1d(xm, conv_w, conv_b))
    q = xc @ w_q
    k = (xc @ w_k) * dh ** -0.5
    v = xm @ w_v
    g = q @ w_gates[0] + k @ w_gates[1] + v @ w_gates[2] + b_gates
    heads = lambda a: a.reshape(bsz, length, ML_HEADS, dh).transpose(0, 2, 1, 3)
    return heads(q), heads(k), heads(v), g.reshape(bsz, length, 2, 2, ML_HEADS), xc, z


def _j_mlstm_out(h, xc, z, gn_g, skip, w_down):
    bsz, nh, length, dh = h.shape
    mu = h.mean(-1, keepdims=True)
    var = jnp.square(h - mu).mean(-1, keepdims=True)
    hn = ((h - mu) * lax.rsqrt(var + LN_EPS)).transpose(0, 2, 1, 3).reshape(bsz, length, nh * dh)
    hn = hn * gn_g + skip * xc
    return (hn * jax.nn.silu(z)) @ w_down


def _j_mlstm_mixer(u_lat, u_ctx, w_up, conv_w, conv_b, w_q, w_k, w_v, w_gates, b_gates, gn_g, skip, w_down):
    lat = _j_mlstm_project(u_lat, w_up, conv_w, conv_b, w_q, w_k, w_v, w_gates, b_gates)
    cxt = _j_mlstm_project(u_ctx, w_up, conv_w, conv_b, w_q, w_k, w_v, w_gates, b_gates)
    bsz = u_lat.shape[0]
    dh = w_q.shape[0] // ML_HEADS
    h_lat = jnp.zeros(lat[0].shape, F32)
    h_ctx = jnp.zeros(cxt[0].shape, F32)
    for direction in range(2):
        rev = (lambda a: jnp.flip(a, axis=2)) if direction == 1 else (lambda a: a)

        def prep(p):
            q, k, v, g = p[:4]
            ig = jnp.moveaxis(g[:, :, direction, 0], 1, 2)
            lf = jax.nn.log_sigmoid(jnp.moveaxis(g[:, :, direction, 1], 1, 2))
            return tuple(rev(a) for a in (q, k, v, ig, lf))

        st0 = (jnp.zeros((bsz, ML_HEADS, dh, dh), F32), jnp.zeros((bsz, ML_HEADS, dh), F32),
               jnp.zeros((bsz, ML_HEADS), F32))
        hc, st_ctx = _j_mlstm_scan(*prep(cxt), st0)
        hl, _ = _j_mlstm_scan(*prep(lat), st_ctx)
        h_ctx = h_ctx + rev(hc)
        h_lat = h_lat + rev(hl)
    return (_j_mlstm_out(h_lat, lat[4], lat[5], gn_g, skip, w_down),
            _j_mlstm_out(h_ctx, cxt[4], cxt[5], gn_g, skip, w_down))


def _j_conformer(u, w_in, b_in, dw_w, dw_b, ln_g, ln_b, w_out, b_out):
    a, gate = jnp.split(u @ w_in + b_in, 2, axis=-1)
    hmid = a * jax.nn.sigmoid(gate)
    hmid = jax.nn.silu(_jln(_j_dwconv1d(hmid, dw_w, dw_b), ln_g, ln_b))
    return hmid @ w_out + b_out


def _grid_transpose(x, rows, cols):
    bsz, _, ch = x.shape
    return x.reshape(bsz, rows, cols, ch).transpose(0, 2, 1, 3).reshape(bsz, rows * cols, ch)


def kernel(x, c, ctx, c_ctx, mod_w, mod_b, post_ln_g, post_ln_b, ffn_w_gate, ffn_w_up, ffn_conv_w, ffn_conv_b, ffn_w_down, s5_lambda_re, s5_lambda_im, s5_log_dt, s5_b_re, s5_b_im, s5_c_re, s5_c_im, s5_d, s5_w_glu, s5_b_glu, ml_w_up, ml_conv_w, ml_conv_b, ml_w_q, ml_w_k, ml_w_v, ml_w_gates, ml_b_gates, ml_gn_g, ml_skip, ml_w_down, cv_w_in, cv_b_in, cv_dw_w, cv_dw_b, cv_ln_g, cv_ln_b, cv_w_out, cv_b_out):
    bsz, length, d = x.shape
    depth = mod_w.shape[0]
    ctx_len = ctx.shape[1]
    rows = length // GRID_W
    alpha = (2 * depth) ** 0.25

    modv = _modulation(c, c_ctx, mod_w, mod_b)
    pad2 = jnp.zeros((2, d), F32)
    h_lat, h_ctx = x, ctx
    for i in range(depth):
        kind, occ = i % N_MIXERS, i // N_MIXERS
        last = i == depth - 1
        mod_l = jnp.concatenate([modv[i, :bsz].reshape(bsz, 6, d), jnp.zeros((bsz, 2, d), F32)], axis=1)
        mod_c = jnp.concatenate([modv[i, bsz].reshape(6, d), pad2], axis=0)[None]
        col_major = (kind != 2) and (occ % 2 == 1)
        ln1 = (post_ln_g[i, 0], post_ln_b[i, 0])
        ffn = (ffn_w_gate[i], ffn_w_up[i], ffn_conv_w[i], ffn_conv_b[i], ffn_w_down[i],
               post_ln_g[i, 1], post_ln_b[i, 1])
        if kind == 0:
            s5 = (s5_lambda_re[occ], s5_lambda_im[occ], s5_log_dt[occ], s5_b_re[occ], s5_b_im[occ],
                  s5_c_re[occ], s5_c_im[occ], s5_d[occ])
            y_lat, y_ctx = _s5_core(h_lat, h_ctx, mod_l, mod_c, s5, col_major)
            h_lat = _glu_ln(y_lat, h_lat, mod_l, s5_w_glu[occ], s5_b_glu[occ], *ln1, alpha=alpha)
            if not last:
                h_ctx = _glu_ln(y_ctx, h_ctx, mod_c, s5_w_glu[occ], s5_b_glu[occ], *ln1, alpha=alpha, mod_row=0)
        elif kind == 1:
            ml = (ml_w_up[occ], ml_conv_w[occ], ml_conv_b[occ], ml_w_q[occ], ml_w_k[occ], ml_w_v[occ],
                  ml_w_gates[occ], ml_b_gates[occ], ml_gn_g[occ], ml_skip[occ], ml_w_down[occ])
            h_in = _grid_transpose(h_lat, rows, GRID_W) if col_major else h_lat
            h_lat, h_ctx = _mlstm_layer(h_in, h_ctx, mod_l, mod_c, ml, *ln1, alpha=alpha, need_ctx=not last)
            if col_major:
                h_lat = _grid_transpose(h_lat, GRID_W, rows)
        else:
            hm_l = _cv_in(h_lat, mod_l, cv_w_in[occ], cv_b_in[occ])
            cv = (cv_dw_w[occ], cv_dw_b[occ], cv_ln_g[occ], cv_ln_b[occ], cv_w_out[occ], cv_b_out[occ])
            if not last:
                hm_c = _cv_in(h_ctx, mod_c, cv_w_in[occ], cv_b_in[occ], mod_row=0)
                h_ctx = _cv_out(hm_c, h_ctx, mod_c, *cv, *ln1, alpha=alpha, mod_row=0)
            h_lat = _cv_out(hm_l, h_lat, mod_l, *cv, *ln1, alpha=alpha)
        h_lat = _conv_ffn_ln(h_lat, mod_l, *ffn, width=GRID_W, alpha=alpha)
        if not last:
            h_ctx = _conv_ffn_ln(h_ctx, mod_c, *ffn, width=ctx_len, alpha=alpha, mod_row=0)
    return h_lat
```

```python
import functools
import math

import jax
import jax.numpy as jnp
from jax import lax
from jax.experimental import pallas as pl
from jax.experimental.pallas import tpu as pltpu

F32 = jnp.float32
BF16 = jnp.bfloat16

GRID_W = 64
SCAN_BLOCK = 128
S5_GROUP = 16
S5_SUB = 16
S5_LANES = 128
ML_HEADS = 4
N_MIXERS = 3
LN_EPS = 1e-5
VMEM_LIMIT = 56 * 1024 * 1024


def _cparams(*sem):
    return pltpu.CompilerParams(dimension_semantics=sem, vmem_limit_bytes=VMEM_LIMIT)


def _const_spec(shape):
    nd = len(shape)
    return pl.BlockSpec(shape, lambda *_: (0,) * nd, pipeline_mode=pl.Buffered(1))


def _ln(x, g, b):
    mu = jnp.mean(x, axis=-1, keepdims=True)
    xc = x - mu
    var = jnp.mean(xc * xc, axis=-1, keepdims=True)
    return xc * lax.rsqrt(var + LN_EPS) * g + b


def _sigmoid(x):
    return 1.0 / (1.0 + jnp.exp(-x))


def _silu(x):
    return x * _sigmoid(x)


def _gelu(x):
    return 0.5 * x * (1.0 + jnp.tanh(math.sqrt(2.0 / math.pi) * (x + 0.044715 * (x * x * x))))


def _bdot(a, b):
    return jnp.dot(a.astype(BF16), b.astype(BF16), preferred_element_type=F32)


def _mod_kernel(c_ref, w_ref, b_ref, o_ref):
    o_ref[0] = _bdot(_silu(c_ref[...]), w_ref[0]) + b_ref[0]


def _modulation(c, c_ctx, mod_w, mod_b):
    depth, d, d6 = mod_w.shape
    bsz = c.shape[0]
    rows = 8
    cs = jnp.zeros((rows, d), F32).at[:bsz].set(c).at[bsz].set(c_ctx)
    tn = d6 // 6
    return pl.pallas_call(
        _mod_kernel,
        grid=(depth, d6 // tn),
        in_specs=[pl.BlockSpec((rows, d), lambda i, j: (0, 0)),
                  pl.BlockSpec((1, d, tn), lambda i, j: (i, 0, j)),
                  pl.BlockSpec((1, 1, tn), lambda i, j: (i, 0, j))],
        out_specs=pl.BlockSpec((1, rows, tn), lambda i, j: (i, 0, j)),
        out_shape=jax.ShapeDtypeStruct((depth, rows, d6), F32),
        compiler_params=_cparams("parallel", "parallel"),
        name="modulation",
    )(cs, mod_w, mod_b.reshape(depth, 1, d6))


def _ffn_kernel(*refs, width, tm, n_chunks, alpha, halo):
    if halo:
        top_ref, x_ref, bot_ref = refs[:3]
        refs = refs[3:]
    else:
        x_ref = refs[0]
        refs = refs[1:]
    mod_ref, wg_ref, wu_ref, cw_ref, cb_ref, wd_ref, lng_ref, lnb_ref, o_ref, xbuf, gbuf, ubuf, abuf, acc = refs
    i = pl.program_id(1)
    nt = pl.num_programs(1)
    pad = 8
    hw = width if halo else 0
    fc = gbuf.shape[-1]

    sh2 = mod_ref[0, 3:4, :]
    sc2 = 1.0 + mod_ref[0, 4:5, :]
    g2 = 1.0 + mod_ref[0, 5:6, :]
    x = x_ref[0]
    xbuf[hw:hw + tm, :] = (x * sc2 + sh2).astype(BF16)
    if halo:
        xbuf[0:hw, :] = (top_ref[0] * sc2 + sh2).astype(BF16)
        xbuf[hw + tm:, :] = (bot_ref[0] * sc2 + sh2).astype(BF16)
    for slot in range(2):
        gbuf[slot, 0:pad, :] = jnp.zeros((pad, fc), F32)
        gbuf[slot, pad + tm + 2 * hw:, :] = jnp.zeros((pad, fc), F32)
    keep_top = (i > 0).astype(F32)
    keep_bot = (i < nt - 1).astype(F32)

    col = lax.broadcasted_iota(jnp.int32, (tm, fc), 0) & (width - 1)
    not_first = col > 0
    not_last = col < width - 1

    def project(j, slot):
        g = jnp.dot(xbuf[...], wg_ref[j], preferred_element_type=F32)
        if halo:
            gbuf[slot, pad:pad + hw, :] = g[0:hw] * keep_top
            gbuf[slot, pad + hw:pad + hw + tm, :] = g[hw:hw + tm]
            gbuf[slot, pad + hw + tm:pad + 2 * hw + tm, :] = g[hw + tm:] * keep_bot
        else:
            gbuf[slot, pad:pad + tm, :] = g
        ubuf[slot] = jnp.dot(xbuf[hw:hw + tm, :], wu_ref[j], preferred_element_type=F32)

    def activate(j, slot):
        def taps(dc):
            t = None
            for dr in ((-1, 0, 1) if halo else (0,)):
                start = pad + hw + dr * width + dc
                w = cw_ref[j, 3 * (dr + 1) + dc + 1:3 * (dr + 1) + dc + 2, :]
                term = gbuf[slot, pl.ds(start, tm), :] * w
                t = term if t is None else t + term
            return t

        gate = (taps(0) + jnp.where(not_first, taps(-1), 0.0) + jnp.where(not_last, taps(1), 0.0)
                + cb_ref[j])
        abuf[slot] = (_gelu(gate) * ubuf[slot]).astype(BF16)

    def contract(j, slot):
        return jnp.dot(abuf[slot], wd_ref[j], preferred_element_type=F32)

    project(0, 0)
    if n_chunks > 1:
        project(1, 1)
    activate(0, 0)

    def step(j, slot):
        acc[...] += contract(j - 1, 1 - slot)
        project(j + 1, 1 - slot)
        activate(j, slot)

    def body(p, carry):
        step(1 + 2 * p, 1)
        step(2 + 2 * p, 0)
        return carry

    acc[...] = jnp.zeros_like(acc)
    pairs = max(n_chunks - 2, 0) // 2
    if pairs:
        lax.fori_loop(0, pairs, body, 0)
    if max(n_chunks - 2, 0) % 2:
        step(1 + 2 * pairs, 1)
    if n_chunks > 1:
        last = n_chunks - 1
        acc[...] += contract(last - 1, 1 - (last & 1))
        activate(last, last & 1)
    y = acc[...] + contract(n_chunks - 1, (n_chunks - 1) & 1)
    o_ref[0] = _ln(alpha * x + g2 * y, lng_ref[...], lnb_ref[...])


def _ffn_chunk(f):
    for fc in (256, 128):
        if f % fc == 0:
            return fc
    raise ValueError(f"ffn hidden size {f} is not a multiple of 128")


def _conv_ffn_ln(h, mod, w_gate, w_up, conv_w, conv_b, w_down, ln_g, ln_b, *, width, alpha, mod_row=None):
    bsz, length, d = h.shape
    f = w_gate.shape[1]
    fc = _ffn_chunk(f)
    nf = f // fc
    rows = length // width
    halo = rows > 1
    tm = min(length, 512) if halo else length
    assert length % tm == 0 and tm % width == 0 and width & (width - 1) == 0
    nt = length // tm
    r = tm // width
    wg = w_gate.astype(BF16).reshape(d, nf, fc).transpose(1, 0, 2)
    wu = w_up.astype(BF16).reshape(d, nf, fc).transpose(1, 0, 2)
    wd = w_down.astype(BF16).reshape(nf, fc, d)
    cw = conv_w.reshape(9, nf, fc).transpose(1, 0, 2)
    cb = conv_b.reshape(nf, 1, fc)
    mod_map = (lambda b, i: (b, 0, 0)) if mod_row is None else (lambda b, i: (mod_row, 0, 0))
    x_spec = pl.BlockSpec((1, tm, d), lambda b, i: (b, i, 0))
    in_specs, args = [x_spec], [h]
    if halo:
        nrow = length // width
        in_specs = [pl.BlockSpec((1, width, d), lambda b, i: (b, jnp.maximum(i * r - 1, 0), 0)), x_spec,
                    pl.BlockSpec((1, width, d), lambda b, i: (b, jnp.minimum((i + 1) * r, nrow - 1), 0))]
        args = [h, h, h]
    in_specs += [pl.BlockSpec((1, 8, d), mod_map), _const_spec(wg.shape), _const_spec(wu.shape),
                 _const_spec(cw.shape), _const_spec(cb.shape), _const_spec(wd.shape),
                 _const_spec((1, d)), _const_spec((1, d))]
    args += [mod, wg, wu, cw, cb, wd, ln_g.reshape(1, d), ln_b.reshape(1, d)]
    hw = width if halo else 0
    return pl.pallas_call(
        functools.partial(_ffn_kernel, width=width, tm=tm, n_chunks=nf, alpha=alpha, halo=halo),
        grid=(bsz, nt),
        in_specs=in_specs,
        out_specs=pl.BlockSpec((1, tm, d), lambda b, i: (b, i, 0)),
        out_shape=jax.ShapeDtypeStruct((bsz, length, d), F32),
        scratch_shapes=[pltpu.VMEM((tm + 2 * hw, d), BF16),
                        pltpu.VMEM((2, tm + 2 * hw + 16, fc), F32),
                        pltpu.VMEM((2, tm, fc), F32),
                        pltpu.VMEM((2, tm, fc), BF16),
                        pltpu.VMEM((tm, d), F32)],
        compiler_params=_cparams("parallel", "parallel"),
        name="conv_ffn_ln",
    )(*args)


def _s5_operators(lam_re, lam_im, log_dt, b_re, b_im, c_re, c_im):
    hi = lax.Precision.HIGHEST
    s = S5_SUB
    e_cols, f_rows, m_tot, a_rows = [], [], None, []
    for direction in range(2):
        lre = jnp.minimum(lam_re[direction], -1e-4)
        lim = lam_im[direction]
        dt = jnp.exp(log_dt[direction])[:, None]
        mag = jnp.exp(lre * dt)
        ar, ai = mag * jnp.cos(lim * dt), mag * jnp.sin(lim * dt)
        nr, ni = ar - 1.0, ai
        den = lre * lre + lim * lim
        cr = (nr * lre + ni * lim) / den
        ci = (ni * lre - nr * lim) / den
        bbr = cr[..., None] * b_re[direction] - ci[..., None] * b_im[direction]
        bbi = cr[..., None] * b_im[direction] + ci[..., None] * b_re[direction]
        pr, pi = [jnp.ones_like(ar)], [jnp.zeros_like(ai)]
        for _ in range(s):
            pr.append(pr[-1] * ar - pi[-1] * ai)
            pi.append(pr[-2] * ai + pi[-1] * ar)
        pr, pi = jnp.stack(pr), jnp.stack(pi)
        abr = pr[:s, :, :, None] * bbr - pi[:s, :, :, None] * bbi
        abi = pr[:s, :, :, None] * bbi + pi[:s, :, :, None] * bbr
        ccr, cci = c_re[direction], c_im[direction]
        kern = (jnp.einsum('gop,tgpi->tgoi', ccr, abr, precision=hi)
                - jnp.einsum('gop,tgpi->tgoi', cci, abi, precision=hi))
        j = jnp.arange(s)[:, None]
        t = jnp.arange(s)[None, :]
        lag = (t - j) if direction == 0 else (j - t)
        m = jnp.where((lag >= 0)[:, :, None, None, None], kern[jnp.clip(lag, 0, s - 1)], 0.0)
        m = m.transpose(2, 0, 4, 1, 3)
        m_tot = m if m_tot is None else m_tot + m
        order = (s - 1 - jnp.arange(s)) if direction == 0 else jnp.arange(s)
        er = abr[order].transpose(1, 0, 3, 2)
        ei = abi[order].transpose(1, 0, 3, 2)
        e_cols.append((er, ei))
        expo = (jnp.arange(s) + 1) if direction == 0 else (s - jnp.arange(s))
        zr, zi = pr[expo], pi[expo]
        f_re = jnp.einsum('gop,tgp->gpto', ccr, zr) - jnp.einsum('gop,tgp->gpto', cci, zi)
        f_im = -(jnp.einsum('gop,tgp->gpto', ccr, zi) + jnp.einsum('gop,tgp->gpto', cci, zr))
        f_rows.append((f_re, f_im))
        a_rows.append((jnp.concatenate([pr[s], pr[s]], -1), jnp.concatenate([-pi[s], pi[s]], -1)))
    g = lam_re.shape[1]
    w = s * S5_GROUP
    (fer, fei), (ber, bei) = e_cols
    e_mat = jnp.concatenate([fer, fei, ber, bei, fei, fer, bei, ber], axis=-1).reshape(g, w, -1)
    (ffr, ffi), (bfr, bfi) = f_rows
    f_mat = jnp.concatenate([ffr, ffi, bfr, bfi], axis=1).reshape(g, -1, w)
    a_mat = jnp.stack([a_rows[0][0], a_rows[0][1], a_rows[1][0], a_rows[1][1]], axis=1)
    return e_mat, m_tot.reshape(g, w, w), f_mat, a_mat


def _pick_group(res, n, gp, bsz):
    if gp == 1:
        return res
    sel = (lax.broadcasted_iota(jnp.int32, (res.shape[0], n), 0) & 7) // bsz
    out = res[:, :n]
    for k in range(1, gp):
        out = jnp.where(sel == k, res[:, k * n:(k + 1) * n], out)
    return out


def _s5_kernel(xc_ref, xl_ref, d_ref, a_ref, e_ref, m_ref, f_ref, yc_ref, yl_ref, se, *, bsz, gp, rc):
    hv = xc_ref.shape[1]
    rows_c, w = xc_ref.shape[2], hv * xc_ref.shape[3]
    rows_l = xl_ref.shape[2]
    ln = w // hv

    def rows_of(x_ref, r0, n):
        return jnp.concatenate([x_ref[0, hf, pl.ds(r0, n), :] for hf in range(hv)], axis=1)
    p2 = a_ref.shape[-1]

    def chunks(rows):
        return [(r0, min(rc, rows - r0)) for r0 in range(0, rows, rc)]

    segments = ((xc_ref, yc_ref, 0, rows_c), (xl_ref, yl_ref, rows_c, rows_l))

    for x_ref, _, base, rows in segments:
        for r0, n in chunks(rows):
            u = rows_of(x_ref, r0, n).astype(BF16)
            e = jnp.dot(u, e_ref[0], preferred_element_type=F32)
            se[base + r0:base + r0 + n, :] = _pick_group(e, 4 * p2, gp, bsz)

    a1f, a2f, a1b, a2b = a_ref[0, 0], a_ref[0, 1], a_ref[0, 2], a_ref[0, 3]

    def make_step(base, tiles):
        def step(s, carry):
            vf, wf, vb, wb = carry
            rf = pl.multiple_of(base + s * 8, 8)
            rb = pl.multiple_of(base + (tiles - 1 - s) * 8, 8)
            ef_v = se[pl.ds(rf, 8), 0:p2]
            ef_w = se[pl.ds(rf, 8), 2 * p2:3 * p2]
            eb_v = se[pl.ds(rb, 8), p2:2 * p2]
            eb_w = se[pl.ds(rb, 8), 3 * p2:4 * p2]
            se[pl.ds(rf, 8), 0:p2] = vf
            se[pl.ds(rb, 8), p2:2 * p2] = vb
            return (a1f * vf + a2f * wf + ef_v, a1f * wf - a2f * vf + ef_w,
                    a1b * vb + a2b * wb + eb_v, a1b * wb - a2b * vb + eb_w)
        return step

    zero = jnp.zeros((8, p2), F32)
    carry = (zero, zero, zero, zero)
    carry = lax.fori_loop(0, rows_c // 8, make_step(0, rows_c // 8), carry, unroll=2)
    lax.fori_loop(0, rows_l // 8, make_step(rows_c, rows_l // 8), carry, unroll=2)

    for x_ref, y_ref, base, rows in segments:
        for r0, n in chunks(rows):
            u = rows_of(x_ref, r0, n)
            s_in = se[base + r0:base + r0 + n, 0:2 * p2].astype(BF16)
            y = (_pick_group(jnp.dot(u.astype(BF16), m_ref[0], preferred_element_type=F32), w, gp, bsz)
                 + _pick_group(jnp.dot(s_in, f_ref[0], preferred_element_type=F32), w, gp, bsz))
            y = y + (u.reshape(n // 8, 8, w) * d_ref[0]).reshape(n, w)
            y = _gelu(y)
            for hf in range(hv):
                y_ref[0, hf, pl.ds(r0, n), :] = y[:, hf * ln:(hf + 1) * ln]


def _chunk_transpose(x):
    n, w = x.shape
    r = lax.broadcasted_iota(jnp.int32, x.shape, 0)
    j = lax.broadcasted_iota(jnp.int32, x.shape, 1) // S5_GROUP
    for m in (1, 2, 4, 8):
        up = pltpu.roll(pltpu.roll(x, n - m, 0), S5_GROUP * m, 1)
        dn = pltpu.roll(pltpu.roll(x, m, 0), w - S5_GROUP * m, 1)
        r_bit = (r & m) != 0
        j_bit = (j & m) != 0
        x = jnp.where(r_bit, jnp.where(j_bit, x, dn), jnp.where(j_bit, up, x))
    return x


def _s5_pack_kernel(x_ref, mod_ref, o_ref, tbuf, *, bsz, gp):
    kb = x_ref.shape[1] // S5_SUB
    lanes = tbuf.shape[-1]
    halves = o_ref.shape[1]
    for b in range(bsz):
        mb = min(b, mod_ref.shape[0] - 1)
        u = x_ref[b] * (1.0 + mod_ref[mb, 1:2, :]) + mod_ref[mb, 0:1, :]
        t = _chunk_transpose(u)
        for hf in range(halves):
            tbuf[b, hf] = t[:, hf * lanes:(hf + 1) * lanes]
    for tile in range(o_ref.shape[0]):
        for gs in range(gp):
            for b in range(bsz):
                for hf in range(halves):
                    o_ref[tile, hf, pl.ds(gs * bsz + b, kb, stride=8), :] = (
                        tbuf[b, hf, pl.ds(tile * gp + gs, kb, stride=S5_SUB), :])


def _s5_unpack_kernel(y_ref, o_ref, tbuf, *, bsz, gp):
    kb = o_ref.shape[1] // S5_SUB
    halves = y_ref.shape[1]
    for tile in range(y_ref.shape[0]):
        for gs in range(gp):
            for b in range(bsz):
                for hf in range(halves):
                    tbuf[b, hf, pl.ds(tile * gp + gs, kb, stride=S5_SUB), :] = (
                        y_ref[tile, hf, pl.ds(gs * bsz + b, kb, stride=8), :])
    for b in range(bsz):
        t = jnp.concatenate([tbuf[b, hf] for hf in range(halves)], axis=1)
        o_ref[b] = _chunk_transpose(t).astype(o_ref.dtype)


def _s5_pack(h, mod, gp):
    bsz, length, d = h.shape
    w = S5_SUB * S5_GROUP
    assert S5_SUB == 16 and S5_GROUP == 16 and d % w == 0
    tb = min(length, 512)
    tiles = S5_SUB // gp
    hv, ln = w // S5_LANES, S5_LANES
    return pl.pallas_call(
        functools.partial(_s5_pack_kernel, bsz=bsz, gp=gp),
        grid=(d // w, length // tb),
        in_specs=[pl.BlockSpec((bsz, tb, w), lambda q, i: (0, i, q)),
                  pl.BlockSpec((mod.shape[0], 8, w), lambda q, i: (0, 0, q))],
        out_specs=pl.BlockSpec((tiles, hv, tb // S5_SUB * 8, ln), lambda q, i: (q, 0, i, 0)),
        out_shape=jax.ShapeDtypeStruct((d // S5_GROUP // gp, hv, length // S5_SUB * 8, ln), F32),
        scratch_shapes=[pltpu.VMEM((bsz, hv, tb, ln), F32)],
        compiler_params=_cparams("parallel", "parallel"),
        name="s5_pack",
    )(h, mod)


def _s5_unpack(y, bsz, gp):
    gt, hv, rows, ln = y.shape
    w = hv * ln
    length = rows // 8 * S5_SUB
    d = gt * gp * S5_GROUP
    tb = min(length, 512)
    tiles = S5_SUB // gp
    return pl.pallas_call(
        functools.partial(_s5_unpack_kernel, bsz=bsz, gp=gp),
        grid=(d // w, length // tb),
        in_specs=[pl.BlockSpec((tiles, hv, tb // S5_SUB * 8, ln), lambda q, i: (q, 0, i, 0))],
        out_specs=pl.BlockSpec((bsz, tb, w), lambda q, i: (0, i, q)),
        out_shape=jax.ShapeDtypeStruct((bsz, length, d), BF16),
        scratch_shapes=[pltpu.VMEM((bsz, hv, tb, ln), F32)],
        compiler_params=_cparams("parallel", "parallel"),
        name="s5_unpack",
    )(y)


def _s5_pattern(v, gp, reps):
    d = v.shape[-1]
    gt = d // S5_GROUP // gp
    x = jnp.broadcast_to(v.reshape(gt, gp, 1, 1, S5_GROUP), (gt, gp, reps, S5_SUB, S5_GROUP))
    return x.reshape(gt, gp * reps, S5_SUB * S5_GROUP)


def _grid_transpose(x, rows, cols):
    bsz, _, ch = x.shape
    return x.reshape(bsz, rows, cols, ch).transpose(0, 2, 1, 3).reshape(bsz, rows * cols, ch)


def _s5_core(h_lat, h_ctx, mod_l, mod_c, params, col_major):
    lam_re, lam_im, log_dt, b_re, b_im, c_re, c_im, d_skip = params
    bsz, length, d = h_lat.shape
    assert 8 % bsz == 0 and S5_SUB % (8 // bsz) == 0
    gp = 8 // bsz
    gt = d // S5_GROUP // gp
    w = S5_SUB * S5_GROUP
    grid_rows = length // GRID_W
    if col_major:
        h_lat = _grid_transpose(h_lat, grid_rows, GRID_W)
    xl = _s5_pack(h_lat, mod_l, gp)
    xc = _s5_pack(h_ctx, mod_c, gp)
    e_mat, m_mat, f_mat, a_mat = _s5_operators(lam_re, lam_im, log_dt, b_re, b_im, c_re, c_im)
    p2 = a_mat.shape[-1]
    cat = lambda m: m.reshape(gt, gp, m.shape[1], m.shape[2]).transpose(0, 2, 1, 3).reshape(gt, m.shape[1], -1)
    e_cat, m_cat, f_cat = (cat(m).astype(BF16) for m in (e_mat, m_mat, f_mat))
    a_pat = jnp.broadcast_to(a_mat.reshape(gt, gp, 1, 4, p2), (gt, gp, bsz, 4, p2)).transpose(0, 3, 1, 2, 4)
    a_pat = a_pat.reshape(gt, 4, 8, p2)
    d_pat = _s5_pattern(d_skip, gp, bsz)
    hv, rows_l, rows_c, ln = xl.shape[1], xl.shape[2], xc.shape[2], xl.shape[3]
    rc = 512
    tile = lambda *shape: pl.BlockSpec((1,) + shape, lambda g: (g,) + (0,) * len(shape))
    yc, yl = pl.pallas_call(
        functools.partial(_s5_kernel, bsz=bsz, gp=gp, rc=rc),
        grid=(gt,),
        in_specs=[tile(hv, rows_c, ln), tile(hv, rows_l, ln), tile(8, w), tile(4, 8, p2),
                  tile(w, gp * 4 * p2), tile(w, gp * w), tile(2 * p2, gp * w)],
        out_specs=[tile(hv, rows_c, ln), tile(hv, rows_l, ln)],
        out_shape=[jax.ShapeDtypeStruct((gt, hv, rows_c, ln), F32),
                   jax.ShapeDtypeStruct((gt, hv, rows_l, ln), F32)],
        scratch_shapes=[pltpu.VMEM((rows_c + rows_l, 4 * p2), F32)],
        compiler_params=_cparams("parallel"),
        name="s5_scan",
    )(xc, xl, d_pat, a_pat, e_cat, m_cat, f_cat)
    y_lat = _s5_unpack(yl, bsz, gp)
    if col_major:
        y_lat = _grid_transpose(y_lat, GRID_W, grid_rows)
    return y_lat, _s5_unpack(yc, bsz, gp)


def _glu_ln_kernel(y_ref, h_ref, mod_ref, w_ref, b_ref, lng_ref, lnb_ref, o_ref, *, alpha):
    d = h_ref.shape[-1]
    z = jnp.dot(y_ref[0], w_ref[...], preferred_element_type=F32) + b_ref[...]
    glu = z[:, :d] * _sigmoid(z[:, d:])
    g1 = 1.0 + mod_ref[0, 2:3, :]
    o_ref[0] = _ln(alpha * h_ref[0] + g1 * glu, lng_ref[...], lnb_ref[...])


def _glu_ln(y, h, mod, w_glu, b_glu, ln_g, ln_b, *, alpha, mod_row=None):
    bsz, length, d = h.shape
    tm = min(length, 512)
    mod_map = (lambda b, i: (b, 0, 0)) if mod_row is None else (lambda b, i: (mod_row, 0, 0))
    tok = lambda: pl.BlockSpec((1, tm, d), lambda b, i: (b, i, 0))
    return pl.pallas_call(
        functools.partial(_glu_ln_kernel, alpha=alpha),
        grid=(bsz, length // tm),
        in_specs=[tok(), tok(), pl.BlockSpec((1, 8, d), mod_map), _const_spec((d, 2 * d)), _const_spec((1, 2 * d)),
                  _const_spec((1, d)), _const_spec((1, d))],
        out_specs=tok(),
        out_shape=jax.ShapeDtypeStruct((bsz, length, d), F32),
        compiler_params=_cparams("parallel", "parallel"),
        name="glu_ln",
    )(y, h, mod, w_glu.astype(BF16), b_glu.reshape(1, 2 * d), ln_g.reshape(1, d), ln_b.reshape(1, d))


def _mod_map(mod_row):
    return (lambda b, i: (b, 0, 0)) if mod_row is None else (lambda b, i: (mod_row, 0, 0))


def _ml_up_kernel(h_ref, mod_ref, w_ref, xm_ref, z_ref):
    e = xm_ref.shape[-1]
    u = (h_ref[0] * (1.0 + mod_ref[0, 1:2, :]) + mod_ref[0, 0:1, :]).astype(BF16)
    y = jnp.dot(u, w_ref[...], preferred_element_type=F32)
    xm_ref[0] = y[:, :e]
    z_ref[0] = y[:, e:].astype(BF16)


def _ml_up(h, mod, w_up, *, mod_row=None):
    bsz, length, d = h.shape
    e = w_up.shape[1] // 2
    tm = min(length, 512)
    tok = lambda n: pl.BlockSpec((1, tm, n), lambda b, i: (b, i, 0))
    return pl.pallas_call(
        _ml_up_kernel,
        grid=(bsz, length // tm),
        in_specs=[tok(d), pl.BlockSpec((1, 8, d), _mod_map(mod_row)), _const_spec((d, 2 * e))],
        out_specs=[tok(e), tok(e)],
        out_shape=[jax.ShapeDtypeStruct((bsz, length, e), F32), jax.ShapeDtypeStruct((bsz, length, e), BF16)],
        compiler_params=_cparams("parallel", "parallel"),
        name="ml_up",
    )(h, mod, w_up.astype(BF16))


def _ml_qkv_kernel(prev_ref, x_ref, next_ref, cw_ref, cb_ref, w_ref, wg_ref, bg_ref,
                   xc_ref, q_ref, k_ref, v_ref, g_ref, xbuf, *, k_scale):
    i = pl.program_id(1)
    nt = pl.num_programs(1)
    tm, e = x_ref.shape[1], x_ref.shape[2]
    hp = prev_ref.shape[1]
    xbuf[0:hp, :] = jnp.where(i > 0, prev_ref[0], 0.0)
    xbuf[hp:hp + tm, :] = x_ref[0]
    xbuf[hp + tm:, :] = jnp.where(i < nt - 1, next_ref[0], 0.0)
    x = x_ref[0]
    conv = (cw_ref[0:1, :] * xbuf[hp - 1:hp - 1 + tm, :] + cw_ref[1:2, :] * x
            + cw_ref[2:3, :] * xbuf[hp + 1:hp + 1 + tm, :] + cb_ref[...])
    xc = _silu(conv).astype(BF16)
    xc_ref[0] = xc
    q = jnp.dot(xc, w_ref[0], preferred_element_type=F32).astype(BF16)
    k = (jnp.dot(xc, w_ref[1], preferred_element_type=F32) * k_scale).astype(BF16)
    v = jnp.dot(x.astype(BF16), w_ref[2], preferred_element_type=F32).astype(BF16)
    q_ref[0] = q
    k_ref[0] = k
    v_ref[0] = v
    g_ref[0] = (jnp.dot(q, wg_ref[0], preferred_element_type=F32) + jnp.dot(k, wg_ref[1], preferred_element_type=F32)
                + jnp.dot(v, wg_ref[2], preferred_element_type=F32) + bg_ref[...])


def _ml_qkv(xm, conv_w, conv_b, w_q, w_k, w_v, w_gates, b_gates):
    bsz, length, e = xm.shape
    tm = min(length, 256)
    hp = 8
    r = tm // hp
    nb = length // hp
    ng = w_gates.shape[-1]
    lanes = 128
    w3 = jnp.stack([w_q, w_k, w_v]).astype(BF16)
    wg = jnp.zeros((3, e, lanes), BF16).at[:, :, :ng].set(w_gates.astype(BF16))
    bg = jnp.zeros((1, lanes), F32).at[0, :ng].set(b_gates)
    cw = jnp.zeros((8, e), F32).at[:conv_w.shape[0]].set(conv_w)
    tok = lambda n: pl.BlockSpec((1, tm, n), lambda b, i: (b, i, 0))
    return pl.pallas_call(
        functools.partial(_ml_qkv_kernel, k_scale=(e // ML_HEADS) ** -0.5),
        grid=(bsz, length // tm),
        in_specs=[pl.BlockSpec((1, hp, e), lambda b, i: (b, jnp.maximum(i * r - 1, 0), 0)), tok(e),
                  pl.BlockSpec((1, hp, e), lambda b, i: (b, jnp.minimum((i + 1) * r, nb - 1), 0)),
                  _const_spec((8, e)), _const_spec((1, e)), _const_spec((3, e, e)), _const_spec((3, e, lanes)),
                  _const_spec((1, lanes))],
        out_specs=[tok(e), tok(e), tok(e), tok(e), tok(lanes)],
        out_shape=[jax.ShapeDtypeStruct((bsz, length, e), BF16)] * 4
                  + [jax.ShapeDtypeStruct((bsz, length, lanes), F32)],
        scratch_shapes=[pltpu.VMEM((tm + 2 * hp, e), F32)],
        compiler_params=_cparams("parallel", "parallel"),
        name="ml_qkv",
    )(xm, xm, xm, cw, conv_b.reshape(1, e), w3, wg, bg)


def _log_sigmoid(x):
    return jnp.minimum(x, 0.0) - jnp.log(1.0 + jnp.exp(-jnp.abs(x)))


def _ml_scan_kernel(*refs, nblk_c, heads):
    (qcf, kcf, vcf, gccf, grcf, qcb, kcb, vcb, gccb, grcb,
     qlf, klf, vlf, gclf, grlf, qlb, klb, vlb, gclb, grlb,
     hcf_ref, hcb_ref, hlf_ref, hlb_ref, c_sc, n_sc, m_sc) = refs
    i = pl.program_id(1)
    is_ctx = i < nblk_c
    t = qcf.shape[1]
    e = qcf.shape[2]
    dh = e // heads

    @pl.when(i == 0)
    def _():
        c_sc[...] = jnp.zeros_like(c_sc)
        n_sc[...] = jnp.zeros_like(n_sc)
        m_sc[...] = jnp.zeros_like(m_sc)

    row = lax.broadcasted_iota(jnp.int32, (t, t), 0)
    col = lax.broadcasted_iota(jnp.int32, (t, t), 1)
    pick = lambda c_ref, l_ref: jnp.where(is_ctx, c_ref[0], l_ref[0])
    outs = []
    for direction, blk in enumerate(((qcf, kcf, vcf, gccf, grcf, qlf, klf, vlf, gclf, grlf),
                                     (qcb, kcb, vcb, gccb, grcb, qlb, klb, vlb, gclb, grlb))):
        q_all, k_all, v_all, gc, gr = (pick(blk[j], blk[j + 5]) for j in range(5))
        mask = (col <= row) if direction == 0 else (col >= row)
        mask_t = (row <= col) if direction == 0 else (row >= col)
        h_heads = []
        for head in range(heads):
            r = direction * heads + head
            ci = direction * 2 * heads + head
            cf = ci + heads
            q = q_all[:, head * dh:(head + 1) * dh]
            k = k_all[:, head * dh:(head + 1) * dh]
            v = v_all[:, head * dh:(head + 1) * dh]
            ig_col, ig_row = gc[:, ci:ci + 1], gr[ci:ci + 1, :]
            lf_col, lf_row = _log_sigmoid(gc[:, cf:cf + 1]), _log_sigmoid(gr[cf:cf + 1, :])
            m_prev = m_sc[r, :, 0:1]
            b_col = jnp.sum(jnp.where(mask, jnp.broadcast_to(lf_row, (t, t)), 0.0), axis=1, keepdims=True)
            b_row = jnp.sum(jnp.where(mask_t, jnp.broadcast_to(lf_col, (t, t)), 0.0), axis=0, keepdims=True)
            b_tot = jnp.sum(lf_row, axis=1, keepdims=True)
            logd = jnp.where(mask, b_col - b_row + ig_row, -jnp.inf)
            m_inter = b_col + m_prev
            m_t = jnp.maximum(m_inter, jnp.max(logd, axis=1, keepdims=True))
            w_inter = jnp.exp(m_inter - m_t)
            qk = lax.dot_general(q, k, (((1,), (1,)), ((), ())), preferred_element_type=F32)
            s = jnp.exp(logd - m_t) * qk
            c_t = c_sc[r]
            n_row = n_sc[r]
            num = (w_inter * jnp.dot(q, c_t.astype(BF16), preferred_element_type=F32)
                   + jnp.dot(s.astype(BF16), v, preferred_element_type=F32))
            den = (w_inter * jnp.sum(q.astype(F32) * n_row, axis=1, keepdims=True)
                   + jnp.sum(s, axis=1, keepdims=True))
            h_heads.append(num / jnp.maximum(jnp.abs(den), jnp.exp(-m_t)))
            decay = b_tot - b_col + ig_col
            m_new = jnp.maximum(b_tot + m_prev, jnp.max(decay, axis=0, keepdims=True))
            w_prev = jnp.exp(b_tot + m_prev - m_new)
            w_r = jnp.exp(decay - m_new)
            wv = (w_r * v.astype(F32)).astype(BF16)
            c_sc[r] = w_prev * c_t + lax.dot_general(k, wv, (((0,), (0,)), ((), ())), preferred_element_type=F32)
            n_sc[r] = w_prev * n_row + jnp.sum(w_r * k.astype(F32), axis=0, keepdims=True)
            m_sc[r] = jnp.broadcast_to(m_new, m_sc.shape[1:])
        outs.append(jnp.concatenate(h_heads, axis=1))

    @pl.when(is_ctx)
    def _():
        hcf_ref[0] = outs[0]
        hcb_ref[0] = outs[1]

    @pl.when(jnp.logical_not(is_ctx))
    def _():
        hlf_ref[0] = outs[0]
        hlb_ref[0] = outs[1]


def _ml_scan(qkv_c, g_c, qkv_l, g_l):
    bsz, len_c, e = qkv_c[0].shape
    len_l = qkv_l[0].shape[1]
    t = SCAN_BLOCK
    nc, nl = len_c // t, len_l // t
    lanes = g_c.shape[-1]
    rows = 16
    grow = lambda g: jnp.swapaxes(g[:, :, :rows], 1, 2)
    cf = lambda i: jnp.minimum(i, nc - 1)
    cb = lambda i: jnp.maximum(nc - 1 - i, 0)
    lf = lambda i: jnp.maximum(i - nc, 0)
    lb = lambda i: jnp.minimum(nl - 1 - (i - nc), nl - 1)
    def specs(idx):
        tok = pl.BlockSpec((1, t, e), lambda b, i: (b, idx(i), 0))
        return [tok, tok, tok, pl.BlockSpec((1, t, lanes), lambda b, i: (b, idx(i), 0)),
                pl.BlockSpec((1, rows, t), lambda b, i: (b, 0, idx(i)))]
    args_c = list(qkv_c) + [g_c, grow(g_c)]
    args_l = list(qkv_l) + [g_l, grow(g_l)]
    out = lambda idx: pl.BlockSpec((1, t, e), lambda b, i: (b, idx(i), 0))
    heads = ML_HEADS
    dh = e // heads
    return pl.pallas_call(
        functools.partial(_ml_scan_kernel, nblk_c=nc, heads=heads),
        grid=(bsz, nc + nl),
        in_specs=specs(cf) + specs(cb) + specs(lf) + specs(lb),
        out_specs=[out(cf), out(cb), out(lf), out(lb)],
        out_shape=[jax.ShapeDtypeStruct((bsz, len_c, e), F32)] * 2 + [jax.ShapeDtypeStruct((bsz, len_l, e), F32)] * 2,
        scratch_shapes=[pltpu.VMEM((2 * heads, dh, dh), F32), pltpu.VMEM((2 * heads, 1, dh), F32),
                        pltpu.VMEM((2 * heads, 1, 128), F32)],
        compiler_params=_cparams("parallel", "arbitrary"),
        name="ml_scan",
    )(*(args_c + args_c + args_l + args_l))


def _ml_out_kernel(hf_ref, hb_ref, xc_ref, z_ref, h_ref, mod_ref, gn_ref, sk_ref, w_ref, lng_ref, lnb_ref, o_ref,
                   *, alpha, heads):
    e = hf_ref.shape[-1]
    dh = e // heads
    hh = hf_ref[0] + hb_ref[0]
    parts = []
    for head in range(heads):
        x = hh[:, head * dh:(head + 1) * dh]
        mu = jnp.mean(x, axis=-1, keepdims=True)
        xc = x - mu
        var = jnp.mean(xc * xc, axis=-1, keepdims=True)
        parts.append(xc * lax.rsqrt(var + LN_EPS))
    hn = jnp.concatenate(parts, axis=1) * gn_ref[...] + sk_ref[...] * xc_ref[0].astype(F32)
    a = (hn * _silu(z_ref[0].astype(F32))).astype(BF16)
    y = jnp.dot(a, w_ref[...], preferred_element_type=F32)
    g1 = 1.0 + mod_ref[0, 2:3, :]
    o_ref[0] = _ln(alpha * h_ref[0] + g1 * y, lng_ref[...], lnb_ref[...])


def _ml_out(hf, hb, xc, z, h, mod, gn_g, skip, w_down, ln_g, ln_b, *, alpha, mod_row=None):
    bsz, length, d = h.shape
    e = hf.shape[-1]
    tm = min(length, 256)
    tok = lambda n: pl.BlockSpec((1, tm, n), lambda b, i: (b, i, 0))
    return pl.pallas_call(
        functools.partial(_ml_out_kernel, alpha=alpha, heads=ML_HEADS),
        grid=(bsz, length // tm),
        in_specs=[tok(e), tok(e), tok(e), tok(e), tok(d), pl.BlockSpec((1, 8, d), _mod_map(mod_row)),
                  _const_spec((1, e)), _const_spec((1, e)), _const_spec((e, d)), _const_spec((1, d)),
                  _const_spec((1, d))],
        out_specs=tok(d),
        out_shape=jax.ShapeDtypeStruct((bsz, length, d), F32),
        compiler_params=_cparams("parallel", "parallel"),
        name="ml_out",
    )(hf, hb, xc, z, h, mod, gn_g.reshape(1, e), skip.reshape(1, e), w_down.astype(BF16),
      ln_g.reshape(1, d), ln_b.reshape(1, d))


def _mlstm_layer(h_lat, h_ctx, mod_l, mod_c, p, ln_g, ln_b, *, alpha, need_ctx):
    (w_up, conv_w, conv_b, w_q, w_k, w_v, w_gates, b_gates, gn_g, skip, w_down) = p
    xm_l, z_l = _ml_up(h_lat, mod_l, w_up)
    xm_c, z_c = _ml_up(h_ctx, mod_c, w_up, mod_row=0)
    xc_l, q_l, k_l, v_l, g_l = _ml_qkv(xm_l, conv_w, conv_b, w_q, w_k, w_v, w_gates, b_gates)
    xc_c, q_c, k_c, v_c, g_c = _ml_qkv(xm_c, conv_w, conv_b, w_q, w_k, w_v, w_gates, b_gates)
    hcf, hcb, hlf, hlb = _ml_scan((q_c, k_c, v_c), g_c, (q_l, k_l, v_l), g_l)
    out_l = _ml_out(hlf, hlb, xc_l, z_l, h_lat, mod_l, gn_g, skip, w_down, ln_g, ln_b, alpha=alpha)
    out_c = None
    if need_ctx:
        out_c = _ml_out(hcf, hcb, xc_c, z_c, h_ctx, mod_c, gn_g, skip, w_down, ln_g, ln_b, alpha=alpha, mod_row=0)
    return out_l, out_c


def _cv_in_kernel(h_ref, mod_ref, w_ref, b_ref, o_ref):
    d = h_ref.shape[-1]
    u = (h_ref[0] * (1.0 + mod_ref[0, 1:2, :]) + mod_ref[0, 0:1, :]).astype(BF16)
    y = jnp.dot(u, w_ref[...], preferred_element_type=F32) + b_ref[...]
    o_ref[0] = y[:, :d] * _sigmoid(y[:, d:])


def _cv_in(h, mod, w_in, b_in, *, mod_row=None):
    bsz, length, d = h.shape
    tm = min(length, 512)
    tok = lambda: pl.BlockSpec((1, tm, d), lambda b, i: (b, i, 0))
    return pl.pallas_call(
        _cv_in_kernel,
        grid=(bsz, length // tm),
        in_specs=[tok(), pl.BlockSpec((1, 8, d), _mod_map(mod_row)), _const_spec((d, 2 * d)),
                  _const_spec((1, 2 * d))],
        out_specs=tok(),
        out_shape=jax.ShapeDtypeStruct((bsz, length, d), F32),
        compiler_params=_cparams("parallel", "parallel"),
        name="cv_in",
    )(h, mod, w_in.astype(BF16), b_in.reshape(1, 2 * d))


def _cv_out_kernel(prev_ref, x_ref, next_ref, h_ref, mod_ref, dw_ref, db_ref, cg_ref, cb_ref, w_ref, b_ref,
                   lng_ref, lnb_ref, o_ref, xbuf, *, alpha, taps):
    i = pl.program_id(1)
    nt = pl.num_programs(1)
    tm = x_ref.shape[1]
    hp = prev_ref.shape[1]
    xbuf[0:hp, :] = jnp.where(i > 0, prev_ref[0], 0.0)
    xbuf[hp:hp + tm, :] = x_ref[0]
    xbuf[hp + tm:, :] = jnp.where(i < nt - 1, next_ref[0], 0.0)
    left = (taps - 1) // 2
    acc = None
    for k in range(taps):
        start = hp + k - left
        term = dw_ref[k:k + 1, :] * xbuf[start:start + tm, :]
        acc = term if acc is None else acc + term
    c = _silu(_ln(acc + db_ref[...], cg_ref[...], cb_ref[...])).astype(BF16)
    y = jnp.dot(c, w_ref[...], preferred_element_type=F32) + b_ref[...]
    g1 = 1.0 + mod_ref[0, 2:3, :]
    o_ref[0] = _ln(alpha * h_ref[0] + g1 * y, lng_ref[...], lnb_ref[...])


def _cv_out(hmid, h, mod, dw_w, dw_b, cln_g, cln_b, w_out, b_out, ln_g, ln_b, *, alpha, mod_row=None):
    bsz, length, d = h.shape
    taps = dw_w.shape[0]
    tm = min(length, 256)
    hp = 16
    assert (taps - 1) // 2 <= hp and taps // 2 <= hp
    r = tm // hp
    nb = length // hp
    dw = jnp.zeros((32, d), F32).at[:taps].set(dw_w)
    tok = lambda: pl.BlockSpec((1, tm, d), lambda b, i: (b, i, 0))
    vec = lambda: _const_spec((1, d))
    return pl.pallas_call(
        functools.partial(_cv_out_kernel, alpha=alpha, taps=taps),
        grid=(bsz, length // tm),
        in_specs=[pl.BlockSpec((1, hp, d), lambda b, i: (b, jnp.maximum(i * r - 1, 0), 0)), tok(),
                  pl.BlockSpec((1, hp, d), lambda b, i: (b, jnp.minimum((i + 1) * r, nb - 1), 0)),
                  tok(), pl.BlockSpec((1, 8, d), _mod_map(mod_row)), _const_spec((32, d)), vec(), vec(), vec(),
                  _const_spec((d, d)), vec(), vec(), vec()],
        out_specs=tok(),
        out_shape=jax.ShapeDtypeStruct((bsz, length, d), F32),
        scratch_shapes=[pltpu.VMEM((tm + 2 * hp, d), F32)],
        compiler_params=_cparams("parallel", "parallel"),
        name="cv_out",
    )(hmid, hmid, hmid, h, mod, dw, dw_b.reshape(1, d), cln_g.reshape(1, d), cln_b.reshape(1, d),
      w_out.astype(BF16), b_out.reshape(1, d), ln_g.reshape(1, d), ln_b.reshape(1, d))


def _jln(x, g, b):
    mu = x.mean(-1, keepdims=True)
    var = jnp.square(x - mu).mean(-1, keepdims=True)
    return (x - mu) * lax.rsqrt(var + LN_EPS) * g + b


def _j_dwconv1d(x, w, b):
    k = w.shape[0]
    y = lax.conv_general_dilated(x, w[:, None, :], (1,), [((k - 1) // 2, k // 2)],
                                 dimension_numbers=('NWC', 'WIO', 'NWC'), feature_group_count=x.shape[-1])
    return y + b


def _j_s5_discretise(lam_re, lam_im, log_dt, b_re, b_im):
    lre = jnp.minimum(lam_re, -1e-4)
    lim = lam_im
    dt = jnp.exp(log_dt)[:, None]
    mag = jnp.exp(lre * dt)
    lb_re, lb_im = mag * jnp.cos(lim * dt), mag * jnp.sin(lim * dt)
    nr, ni = lb_re - 1.0, lb_im
    den = lre * lre + lim * lim
    cr = (nr * lre + ni * lim) / den
    ci = (ni * lre - nr * lim) / den
    bb_re = cr[..., None] * b_re - ci[..., None] * b_im
    bb_im = cr[..., None] * b_im + ci[..., None] * b_re
    return lb_re, lb_im, bb_re, bb_im


def _j_combine(e1, e2):
    a1r, a1i, b1r, b1i = e1
    a2r, a2i, b2r, b2i = e2
    return (a2r * a1r - a2i * a1i, a2r * a1i + a2i * a1r,
            a2r * b1r - a2i * b1i + b2r, a2r * b1i + a2i * b1r + b2i)


def _j_s5_scan(u, lb_re, lb_im, bb_re, bb_im, c_re, c_im, s0):
    bsz, length, d = u.shape
    groups = d // S5_GROUP
    nblk = length // 128
    ub = jnp.moveaxis(u.reshape(bsz, nblk, 128, groups, S5_GROUP), 1, 0)

    def step(carry, u_blk):
        sr, si = carry
        bu_re = jnp.einsum('btgs,gps->btgp', u_blk, bb_re)
        bu_im = jnp.einsum('btgs,gps->btgp', u_blk, bb_im)
        a_re = jnp.broadcast_to(lb_re, bu_re.shape)
        a_im = jnp.broadcast_to(lb_im, bu_im.shape)
        pa_re, pa_im, pb_re, pb_im = lax.associative_scan(_j_combine, (a_re, a_im, bu_re, bu_im), axis=1)
        st_re = pa_re * sr[:, None] - pa_im * si[:, None] + pb_re
        st_im = pa_re * si[:, None] + pa_im * sr[:, None] + pb_im
        y = jnp.einsum('gsp,btgp->btgs', c_re, st_re) - jnp.einsum('gsp,btgp->btgs', c_im, st_im)
        return (st_re[:, -1], st_im[:, -1]), y.reshape(bsz, 128, d)

    s_end, ys = lax.scan(step, s0, ub)
    return jnp.moveaxis(ys, 0, 1).reshape(bsz, length, d), s_end


def _j_s5_mixer(ul, uc, lam_re, lam_im, log_dt, b_re, b_im, c_re, c_im, d_skip, w_glu, b_glu):
    y_lat, y_ctx = d_skip * ul, d_skip * uc
    bsz = ul.shape[0]
    groups, p = lam_re.shape[1], lam_re.shape[2]
    for direction in range(2):
        lb_re, lb_im, bb_re, bb_im = _j_s5_discretise(lam_re[direction], lam_im[direction], log_dt[direction],
                                                      b_re[direction], b_im[direction])
        cr, ci = c_re[direction], c_im[direction]
        zero = jnp.zeros((bsz, groups, p), F32)
        rev = (lambda a: a[:, ::-1]) if direction == 1 else (lambda a: a)
        yc, s_ctx = _j_s5_scan(rev(uc), lb_re, lb_im, bb_re, bb_im, cr, ci, (zero, zero))
        yl, _ = _j_s5_scan(rev(ul), lb_re, lb_im, bb_re, bb_im, cr, ci, s_ctx)
        y_ctx = y_ctx + rev(yc)
        y_lat = y_lat + rev(yl)

    def glu(y):
        z = jax.nn.gelu(y) @ w_glu + b_glu
        val, gate = jnp.split(z, 2, axis=-1)
        return val * jax.nn.sigmoid(gate)

    return glu(y_lat), glu(y_ctx)


def _j_mlstm_scan(q, k, v, ig, lf, state):
    bsz, nh, length, dh = q.shape
    nblk = length // SCAN_BLOCK
    blocks = lambda a: jnp.moveaxis(a.reshape(bsz, nh, nblk, SCAN_BLOCK, *a.shape[3:]), 2, 0)
    tri = jnp.tril(jnp.ones((SCAN_BLOCK, SCAN_BLOCK), bool))

    def step(carry, blk):
        C, n, m = carry
        qb, kb, vb, ib, fb = blk
        b = jnp.cumsum(fb, axis=-1)
        logd = jnp.where(tri, b[..., :, None] - b[..., None, :] + ib[..., None, :], -jnp.inf)
        m_inter = b + m[..., None]
        m_t = jnp.maximum(m_inter, logd.max(-1))
        w_inter = jnp.exp(m_inter - m_t)
        s = jnp.exp(logd - m_t[..., None]) * jnp.einsum('bhtd,bhsd->bhts', qb, kb)
        num = w_inter[..., None] * jnp.einsum('bhed,bhtd->bhte', C, qb) + jnp.einsum('bhts,bhse->bhte', s, vb)
        den = w_inter * jnp.einsum('bhd,bhtd->bht', n, qb) + s.sum(-1)
        h = num / jnp.maximum(jnp.abs(den), jnp.exp(-m_t))[..., None]
        decay = b[..., -1:] - b + ib
        m_new = jnp.maximum(b[..., -1] + m, decay.max(-1))
        w_prev = jnp.exp(b[..., -1] + m - m_new)
        w_r = jnp.exp(decay - m_new[..., None])
        C = w_prev[..., None, None] * C + jnp.einsum('bhse,bhsd->bhed', w_r[..., None] * vb, kb)
        n = w_prev[..., None] * n + jnp.einsum('bhs,bhsd->bhd', w_r, kb)
        return (C, n, m_new), h

    state, hs = lax.scan(step, state, tuple(blocks(a) for a in (q, k, v, ig, lf)))
    return jnp.moveaxis(hs, 0, 2).reshape(bsz, nh, length, dh), state


def _j_mlstm_project(u, w_up, conv_w, conv_b, w_q, w_k, w_v, w_gates, b_gates):
    bsz, length, _ = u.shape
    e = w_q.shape[0]
    dh = e // ML_HEADS
    xm, z = jnp.split(u @ w_up, 2, axis=-1)
    xc = jax.nn.silu(_j_dwconv1d(xm, conv_w, conv_b))
    q = xc @ w_q
    k = (xc @ w_k) * dh ** -0.5
    v = xm @ w_v
    g = q @ w_gates[0] + k @ w_gates[1] + v @ w_gates[2] + b_gates
    heads = lambda a: a.reshape(bsz, length, ML_HEADS, dh).transpose(0, 2, 1, 3)
    return heads(q), heads(k), heads(v), g.reshape(bsz, length, 2, 2, ML_HEADS), xc, z


def _j_mlstm_out(h, xc, z, gn_g, skip, w_down):
    bsz, nh, length, dh = h.shape
    mu = h.mean(-1, keepdims=True)
    var = jnp.square(h - mu).mean(-1, keepdims=True)
    hn = ((h - mu) * lax.rsqrt(var + LN_EPS)).transpose(0, 2, 1, 3).reshape(bsz, length, nh * dh)
    hn = hn * gn_g + skip * xc
    return (hn * jax.nn.silu(z)) @ w_down


def _j_mlstm_mixer(u_lat, u_ctx, w_up, conv_w, conv_b, w_q, w_k, w_v, w_gates, b_gates, gn_g, skip, w_down):
    lat = _j_mlstm_project(u_lat, w_up, conv_w, conv_b, w_q, w_k, w_v, w_gates, b_gates)
    cxt = _j_mlstm_project(u_ctx, w_up, conv_w, conv_b, w_q, w_k, w_v, w_gates, b_gates)
    bsz = u_lat.shape[0]
    dh = w_q.shape[0] // ML_HEADS
    h_lat = jnp.zeros(lat[0].shape, F32)
    h_ctx = jnp.zeros(cxt[0].shape, F32)
    for direction in range(2):
        rev = (lambda a: jnp.flip(a, axis=2)) if direction == 1 else (lambda a: a)

        def prep(p):
            q, k, v, g = p[:4]
            ig = jnp.moveaxis(g[:, :, direction, 0], 1, 2)
            lf = jax.nn.log_sigmoid(jnp.moveaxis(g[:, :, direction, 1], 1, 2))
            return tuple(rev(a) for a in (q, k, v, ig, lf))

        st0 = (jnp.zeros((bsz, ML_HEADS, dh, dh), F32), jnp.zeros((bsz, ML_HEADS, dh), F32),
               jnp.zeros((bsz, ML_HEADS), F32))
        hc, st_ctx = _j_mlstm_scan(*prep(cxt), st0)
        hl, _ = _j_mlstm_scan(*prep(lat), st_ctx)
        h_ctx = h_ctx + rev(hc)
        h_lat = h_lat + rev(hl)
    return (_j_mlstm_out(h_lat, lat[4], lat[5], gn_g, skip, w_down),
            _j_mlstm_out(h_ctx, cxt[4], cxt[5], gn_g, skip, w_down))


def _j_conformer(u, w_in, b_in, dw_w, dw_b, ln_g, ln_b, w_out, b_out):
    a, gate = jnp.split(u @ w_in + b_in, 2, axis=-1)
    hmid = a * jax.nn.sigmoid(gate)
    hmid = jax.nn.silu(_jln(_j_dwconv1d(hmid, dw_w, dw_b), ln_g, ln_b))
    return hmid @ w_out + b_out


def _grid_transpose(x, rows, cols):
    bsz, _, ch = x.shape
    return x.reshape(bsz, rows, cols, ch).transpose(0, 2, 1, 3).reshape(bsz, rows * cols, ch)


def kernel(x, c, ctx, c_ctx, mod_w, mod_b, post_ln_g, post_ln_b, ffn_w_gate, ffn_w_up, ffn_conv_w, ffn_conv_b, ffn_w_down, s5_lambda_re, s5_lambda_im, s5_log_dt, s5_b_re, s5_b_im, s5_c_re, s5_c_im, s5_d, s5_w_glu, s5_b_glu, ml_w_up, ml_conv_w, ml_conv_b, ml_w_q, ml_w_k, ml_w_v, ml_w_gates, ml_b_gates, ml_gn_g, ml_skip, ml_w_down, cv_w_in, cv_b_in, cv_dw_w, cv_dw_b, cv_ln_g, cv_ln_b, cv_w_out, cv_b_out):
    bsz, length, d = x.shape
    depth = mod_w.shape[0]
    ctx_len = ctx.shape[1]
    rows = length // GRID_W
    alpha = (2 * depth) ** 0.25

    modv = _modulation(c, c_ctx, mod_w, mod_b)
    pad2 = jnp.zeros((2, d), F32)
    h_lat, h_ctx = x, ctx
    for i in range(depth):
        kind, occ = i % N_MIXERS, i // N_MIXERS
        last = i == depth - 1
        mod_l = jnp.concatenate([modv[i, :bsz].reshape(bsz, 6, d), jnp.zeros((bsz, 2, d), F32)], axis=1)
        mod_c = jnp.concatenate([modv[i, bsz].reshape(6, d), pad2], axis=0)[None]
        col_major = (kind != 2) and (occ % 2 == 1)
        ln1 = (post_ln_g[i, 0], post_ln_b[i, 0])
        ffn = (ffn_w_gate[i], ffn_w_up[i], ffn_conv_w[i], ffn_conv_b[i], ffn_w_down[i],
               post_ln_g[i, 1], post_ln_b[i, 1])
        if kind == 0:
            s5 = (s5_lambda_re[occ], s5_lambda_im[occ], s5_log_dt[occ], s5_b_re[occ], s5_b_im[occ],
                  s5_c_re[occ], s5_c_im[occ], s5_d[occ])
            y_lat, y_ctx = _s5_core(h_lat, h_ctx, mod_l, mod_c, s5, col_major)
            h_lat = _glu_ln(y_lat, h_lat, mod_l, s5_w_glu[occ], s5_b_glu[occ], *ln1, alpha=alpha)
            if not last:
                h_ctx = _glu_ln(y_ctx, h_ctx, mod_c, s5_w_glu[occ], s5_b_glu[occ], *ln1, alpha=alpha, mod_row=0)
        elif kind == 1:
            ml = (ml_w_up[occ], ml_conv_w[occ], ml_conv_b[occ], ml_w_q[occ], ml_w_k[occ], ml_w_v[occ],
                  ml_w_gates[occ], ml_b_gates[occ], ml_gn_g[occ], ml_skip[occ], ml_w_down[occ])
            h_in = _grid_transpose(h_lat, rows, GRID_W) if col_major else h_lat
            h_lat, h_ctx = _mlstm_layer(h_in, h_ctx, mod_l, mod_c, ml, *ln1, alpha=alpha, need_ctx=not last)
            if col_major:
                h_lat = _grid_transpose(h_lat, GRID_W, rows)
        else:
            hm_l = _cv_in(h_lat, mod_l, cv_w_in[occ], cv_b_in[occ])
            cv = (cv_dw_w[occ], cv_dw_b[occ], cv_ln_g[occ], cv_ln_b[occ], cv_w_out[occ], cv_b_out[occ])
            if not last:
                hm_c = _cv_in(h_ctx, mod_c, cv_w_in[occ], cv_b_in[occ], mod_row=0)
                h_ctx = _cv_out(hm_c, h_ctx, mod_c, *cv, *ln1, alpha=alpha, mod_row=0)
            h_lat = _cv_out(hm_l, h_lat, mod_l, *cv, *ln1, alpha=alpha)
        h_lat = _conv_ffn_ln(h_lat, mod_l, *ffn, width=GRID_W, alpha=alpha)
        if not last:
            h_ctx = _conv_ffn_ln(h_ctx, mod_c, *ffn, width=ctx_len, alpha=alpha, mod_row=0)
    return h_lat
```

```python
import functools
import math

import jax
import jax.numpy as jnp
import numpy as np
from jax import lax
from jax.experimental import pallas as pl
from jax.experimental.pallas import tpu as pltpu

F32 = jnp.float32
BF16 = jnp.bfloat16

GRID_W = 64
SCAN_BLOCK = 128
S5_GROUP = 16
S5_SUB = 16
S5_LANES = 128
ML_HEADS = 4
N_MIXERS = 3
LN_EPS = 1e-5
VMEM_LIMIT = 56 * 1024 * 1024


def _cparams(*sem):
    return pltpu.CompilerParams(dimension_semantics=sem, vmem_limit_bytes=VMEM_LIMIT)


def _const_spec(shape):
    nd = len(shape)
    return pl.BlockSpec(shape, lambda *_: (0,) * nd, pipeline_mode=pl.Buffered(1))


def _ln(x, g, b):
    mu = jnp.mean(x, axis=-1, keepdims=True)
    xc = x - mu
    var = jnp.mean(xc * xc, axis=-1, keepdims=True)
    return xc * lax.rsqrt(var + LN_EPS) * g + b


def _sigmoid(x):
    return 1.0 / (1.0 + jnp.exp(-x))


def _silu(x):
    return x * _sigmoid(x)


def _gelu(x):
    return 0.5 * x * (1.0 + jnp.tanh(math.sqrt(2.0 / math.pi) * (x + 0.044715 * (x * x * x))))


def _bdot(a, b):
    return jnp.dot(a.astype(BF16), b.astype(BF16), preferred_element_type=F32)


def _mod_kernel(c_ref, w_ref, b_ref, o_ref):
    o_ref[0] = _bdot(_silu(c_ref[...]), w_ref[0]) + b_ref[0]


def _modulation(c, c_ctx, mod_w, mod_b):
    depth, d, d6 = mod_w.shape
    bsz = c.shape[0]
    rows = 8
    cs = jnp.zeros((rows, d), F32).at[:bsz].set(c).at[bsz].set(c_ctx)
    tn = d6 // 6
    return pl.pallas_call(
        _mod_kernel,
        grid=(depth, d6 // tn),
        in_specs=[pl.BlockSpec((rows, d), lambda i, j: (0, 0)),
                  pl.BlockSpec((1, d, tn), lambda i, j: (i, 0, j)),
                  pl.BlockSpec((1, 1, tn), lambda i, j: (i, 0, j))],
        out_specs=pl.BlockSpec((1, rows, tn), lambda i, j: (i, 0, j)),
        out_shape=jax.ShapeDtypeStruct((depth, rows, d6), F32),
        compiler_params=_cparams("parallel", "parallel"),
        name="modulation",
    )(cs, mod_w, mod_b.reshape(depth, 1, d6))


def _ffn_kernel(*refs, width, tm, n_chunks, alpha, halo):
    if halo:
        top_ref, x_ref, bot_ref = refs[:3]
        refs = refs[3:]
    else:
        x_ref = refs[0]
        refs = refs[1:]
    mod_ref, wg_ref, wu_ref, cw_ref, cb_ref, wd_ref, lng_ref, lnb_ref, o_ref, xbuf, gbuf, ubuf, abuf, acc = refs
    i = pl.program_id(1)
    nt = pl.num_programs(1)
    pad = 8
    hw = width if halo else 0
    fc = gbuf.shape[-1]

    sh2 = mod_ref[0, 3:4, :]
    sc2 = 1.0 + mod_ref[0, 4:5, :]
    g2 = 1.0 + mod_ref[0, 5:6, :]
    x = x_ref[0]
    xbuf[hw:hw + tm, :] = (x * sc2 + sh2).astype(BF16)
    if halo:
        xbuf[0:hw, :] = (top_ref[0] * sc2 + sh2).astype(BF16)
        xbuf[hw + tm:, :] = (bot_ref[0] * sc2 + sh2).astype(BF16)
    for slot in range(2):
        gbuf[slot, 0:pad, :] = jnp.zeros((pad, fc), F32)
        gbuf[slot, pad + tm + 2 * hw:, :] = jnp.zeros((pad, fc), F32)
    keep_top = (i > 0).astype(F32)
    keep_bot = (i < nt - 1).astype(F32)

    col = lax.broadcasted_iota(jnp.int32, (tm, fc), 0) & (width - 1)
    not_first = col > 0
    not_last = col < width - 1

    def project(j, slot):
        g = jnp.dot(xbuf[...], wg_ref[j], preferred_element_type=F32)
        if halo:
            gbuf[slot, pad:pad + hw, :] = g[0:hw] * keep_top
            gbuf[slot, pad + hw:pad + hw + tm, :] = g[hw:hw + tm]
            gbuf[slot, pad + hw + tm:pad + 2 * hw + tm, :] = g[hw + tm:] * keep_bot
        else:
            gbuf[slot, pad:pad + tm, :] = g
        ubuf[slot] = jnp.dot(xbuf[hw:hw + tm, :], wu_ref[j], preferred_element_type=F32)

    def activate(j, slot):
        def taps(dc):
            t = None
            for dr in ((-1, 0, 1) if halo else (0,)):
                start = pad + hw + dr * width + dc
                w = cw_ref[j, 3 * (dr + 1) + dc + 1:3 * (dr + 1) + dc + 2, :]
                term = gbuf[slot, pl.ds(start, tm), :] * w
                t = term if t is None else t + term
            return t

        gate = (taps(0) + jnp.where(not_first, taps(-1), 0.0) + jnp.where(not_last, taps(1), 0.0)
                + cb_ref[j])
        abuf[slot] = (_gelu(gate) * ubuf[slot]).astype(BF16)

    def contract(j, slot):
        return jnp.dot(abuf[slot], wd_ref[j], preferred_element_type=F32)

    project(0, 0)
    if n_chunks > 1:
        project(1, 1)
    activate(0, 0)

    def step(j, slot):
        acc[...] += contract(j - 1, 1 - slot)
        project(j + 1, 1 - slot)
        activate(j, slot)

    def body(p, carry):
        step(1 + 2 * p, 1)
        step(2 + 2 * p, 0)
        return carry

    acc[...] = jnp.zeros_like(acc)
    pairs = max(n_chunks - 2, 0) // 2
    if pairs:
        lax.fori_loop(0, pairs, body, 0)
    if max(n_chunks - 2, 0) % 2:
        step(1 + 2 * pairs, 1)
    if n_chunks > 1:
        last = n_chunks - 1
        acc[...] += contract(last - 1, 1 - (last & 1))
        activate(last, last & 1)
    y = acc[...] + contract(n_chunks - 1, (n_chunks - 1) & 1)
    o_ref[0] = _ln(alpha * x + g2 * y, lng_ref[...], lnb_ref[...])


def _ffn_chunk(f):
    for fc in (256, 128):
        if f % fc == 0:
            return fc
    raise ValueError(f"ffn hidden size {f} is not a multiple of 128")


def _conv_ffn_ln(h, mod, w_gate, w_up, conv_w, conv_b, w_down, ln_g, ln_b, *, width, alpha, mod_row=None):
    bsz, length, d = h.shape
    f = w_gate.shape[1]
    fc = _ffn_chunk(f)
    nf = f // fc
    rows = length // width
    halo = rows > 1
    tm = min(length, 512) if halo else length
    assert length % tm == 0 and tm % width == 0 and width & (width - 1) == 0
    nt = length // tm
    r = tm // width
    wg = w_gate.astype(BF16).reshape(d, nf, fc).transpose(1, 0, 2)
    wu = w_up.astype(BF16).reshape(d, nf, fc).transpose(1, 0, 2)
    wd = w_down.astype(BF16).reshape(nf, fc, d)
    cw = conv_w.reshape(9, nf, fc).transpose(1, 0, 2)
    cb = conv_b.reshape(nf, 1, fc)
    mod_map = (lambda b, i: (b, 0, 0)) if mod_row is None else (lambda b, i: (mod_row, 0, 0))
    x_spec = pl.BlockSpec((1, tm, d), lambda b, i: (b, i, 0))
    in_specs, args = [x_spec], [h]
    if halo:
        nrow = length // width
        in_specs = [pl.BlockSpec((1, width, d), lambda b, i: (b, jnp.maximum(i * r - 1, 0), 0)), x_spec,
                    pl.BlockSpec((1, width, d), lambda b, i: (b, jnp.minimum((i + 1) * r, nrow - 1), 0))]
        args = [h, h, h]
    in_specs += [pl.BlockSpec((1, 8, d), mod_map), _const_spec(wg.shape), _const_spec(wu.shape),
                 _const_spec(cw.shape), _const_spec(cb.shape), _const_spec(wd.shape),
                 _const_spec((1, d)), _const_spec((1, d))]
    args += [mod, wg, wu, cw, cb, wd, ln_g.reshape(1, d), ln_b.reshape(1, d)]
    hw = width if halo else 0
    return pl.pallas_call(
        functools.partial(_ffn_kernel, width=width, tm=tm, n_chunks=nf, alpha=alpha, halo=halo),
        grid=(bsz, nt),
        in_specs=in_specs,
        out_specs=pl.BlockSpec((1, tm, d), lambda b, i: (b, i, 0)),
        out_shape=jax.ShapeDtypeStruct((bsz, length, d), F32),
        scratch_shapes=[pltpu.VMEM((tm + 2 * hw, d), BF16),
                        pltpu.VMEM((2, tm + 2 * hw + 16, fc), F32),
                        pltpu.VMEM((2, tm, fc), F32),
                        pltpu.VMEM((2, tm, fc), BF16),
                        pltpu.VMEM((tm, d), F32)],
        compiler_params=_cparams("parallel", "parallel"),
        name="conv_ffn_ln",
    )(*args)


def _s5_operators(lam_re, lam_im, log_dt, b_re, b_im, c_re, c_im):
    hi = lax.Precision.HIGHEST
    s = S5_SUB
    e_cols, f_rows, m_tot, a_rows = [], [], None, []
    for direction in range(2):
        lre = jnp.minimum(lam_re[direction], -1e-4)
        lim = lam_im[direction]
        dt = jnp.exp(log_dt[direction])[:, None]
        mag = jnp.exp(lre * dt)
        ar, ai = mag * jnp.cos(lim * dt), mag * jnp.sin(lim * dt)
        nr, ni = ar - 1.0, ai
        den = lre * lre + lim * lim
        cr = (nr * lre + ni * lim) / den
        ci = (ni * lre - nr * lim) / den
        bbr = cr[..., None] * b_re[direction] - ci[..., None] * b_im[direction]
        bbi = cr[..., None] * b_im[direction] + ci[..., None] * b_re[direction]
        pr, pi = [jnp.ones_like(ar)], [jnp.zeros_like(ai)]
        for _ in range(s):
            pr.append(pr[-1] * ar - pi[-1] * ai)
            pi.append(pr[-2] * ai + pi[-1] * ar)
        pr, pi = jnp.stack(pr), jnp.stack(pi)
        abr = pr[:s, :, :, None] * bbr - pi[:s, :, :, None] * bbi
        abi = pr[:s, :, :, None] * bbi + pi[:s, :, :, None] * bbr
        ccr, cci = c_re[direction], c_im[direction]
        kern = (jnp.einsum('gop,tgpi->tgoi', ccr, abr, precision=hi)
                - jnp.einsum('gop,tgpi->tgoi', cci, abi, precision=hi))
        j = np.arange(s)[:, None]
        t = np.arange(s)[None, :]
        lag = (t - j) if direction == 0 else (j - t)
        place = jnp.asarray(lag[:, :, None] == np.arange(s), F32)
        m = jnp.einsum('jtz,zgoi->gjito', place, kern, precision=hi)
        m_tot = m if m_tot is None else m_tot + m
        flip = (lambda a: a[::-1]) if direction == 0 else (lambda a: a)
        er = flip(abr).transpose(1, 0, 3, 2)
        ei = flip(abi).transpose(1, 0, 3, 2)
        e_cols.append((er, ei))
        zr, zi = (pr[1:], pi[1:]) if direction == 0 else (pr[:0:-1], pi[:0:-1])
        f_re = jnp.einsum('gop,tgp->gpto', ccr, zr) - jnp.einsum('gop,tgp->gpto', cci, zi)
        f_im = -(jnp.einsum('gop,tgp->gpto', ccr, zi) + jnp.einsum('gop,tgp->gpto', cci, zr))
        f_rows.append((f_re, f_im))
        a_rows.append((jnp.concatenate([pr[s], pr[s]], -1), jnp.concatenate([-pi[s], pi[s]], -1)))
    g = lam_re.shape[1]
    w = s * S5_GROUP
    (fer, fei), (ber, bei) = e_cols
    e_mat = jnp.concatenate([fer, fei, ber, bei, fei, fer, bei, ber], axis=-1).reshape(g, w, -1)
    (ffr, ffi), (bfr, bfi) = f_rows
    f_mat = jnp.concatenate([ffr, ffi, bfr, bfi], axis=1).reshape(g, -1, w)
    a_mat = jnp.stack([a_rows[0][0], a_rows[0][1], a_rows[1][0], a_rows[1][1]], axis=1)
    return e_mat, m_tot.reshape(g, w, w), f_mat, a_mat


def _pick_group(res, n, gp, bsz):
    if gp == 1:
        return res
    sel = (lax.broadcasted_iota(jnp.int32, (res.shape[0], n), 0) & 7) // bsz
    out = res[:, :n]
    for k in range(1, gp):
        out = jnp.where(sel == k, res[:, k * n:(k + 1) * n], out)
    return out


def _s5_kernel(xc_ref, xl_ref, d_ref, a_ref, e_ref, m_ref, f_ref, yc_ref, yl_ref, se, *, bsz, gp, rc):
    hv = xc_ref.shape[1]
    rows_c, w = xc_ref.shape[2], hv * xc_ref.shape[3]
    rows_l = xl_ref.shape[2]
    ln = w // hv

    def rows_of(x_ref, r0, n):
        return jnp.concatenate([x_ref[0, hf, pl.ds(r0, n), :] for hf in range(hv)], axis=1)
    p2 = a_ref.shape[-1]

    def chunks(rows):
        return [(r0, min(rc, rows - r0)) for r0 in range(0, rows, rc)]

    segments = ((xc_ref, yc_ref, 0, rows_c), (xl_ref, yl_ref, rows_c, rows_l))

    for x_ref, _, base, rows in segments:
        for r0, n in chunks(rows):
            u = rows_of(x_ref, r0, n).astype(BF16)
            e = jnp.dot(u, e_ref[0], preferred_element_type=F32)
            se[base + r0:base + r0 + n, :] = _pick_group(e, 4 * p2, gp, bsz)

    a1f, a2f, a1b, a2b = a_ref[0, 0], a_ref[0, 1], a_ref[0, 2], a_ref[0, 3]

    def make_step(base, tiles):
        def step(s, carry):
            vf, wf, vb, wb = carry
            rf = pl.multiple_of(base + s * 8, 8)
            rb = pl.multiple_of(base + (tiles - 1 - s) * 8, 8)
            ef_v = se[pl.ds(rf, 8), 0:p2]
            ef_w = se[pl.ds(rf, 8), 2 * p2:3 * p2]
            eb_v = se[pl.ds(rb, 8), p2:2 * p2]
            eb_w = se[pl.ds(rb, 8), 3 * p2:4 * p2]
            se[pl.ds(rf, 8), 0:p2] = vf
            se[pl.ds(rb, 8), p2:2 * p2] = vb
            return (a1f * vf + a2f * wf + ef_v, a1f * wf - a2f * vf + ef_w,
                    a1b * vb + a2b * wb + eb_v, a1b * wb - a2b * vb + eb_w)
        return step

    zero = jnp.zeros((8, p2), F32)
    carry = (zero, zero, zero, zero)
    carry = lax.fori_loop(0, rows_c // 8, make_step(0, rows_c // 8), carry, unroll=2)
    lax.fori_loop(0, rows_l // 8, make_step(rows_c, rows_l // 8), carry, unroll=2)

    for x_ref, y_ref, base, rows in segments:
        for r0, n in chunks(rows):
            u = rows_of(x_ref, r0, n)
            s_in = se[base + r0:base + r0 + n, 0:2 * p2].astype(BF16)
            y = (_pick_group(jnp.dot(u.astype(BF16), m_ref[0], preferred_element_type=F32), w, gp, bsz)
                 + _pick_group(jnp.dot(s_in, f_ref[0], preferred_element_type=F32), w, gp, bsz))
            y = y + (u.reshape(n // 8, 8, w) * d_ref[0]).reshape(n, w)
            y = _gelu(y)
            for hf in range(hv):
                y_ref[0, hf, pl.ds(r0, n), :] = y[:, hf * ln:(hf + 1) * ln]


def _chunk_butterfly(lo, hi):
    n = len(lo)
    lanes = lo[0].shape[1]
    chunk = lax.broadcasted_iota(jnp.int32, lo[0].shape, 1) // S5_GROUP
    m = 1
    while m < n // 2:
        odd = (chunk & m) != 0

        def exchange(v, m=m, odd=odd):
            out = []
            for t in range(n):
                if t & m == 0:
                    out.append(jnp.where(odd, pltpu.roll(v[t + m], S5_GROUP * m, 1), v[t]))
                else:
                    out.append(jnp.where(odd, v[t], pltpu.roll(v[t - m], lanes - S5_GROUP * m, 1)))
            return out

        lo, hi = exchange(lo), exchange(hi)
        m *= 2
    h = n // 2
    return ([lo[t] if t < h else hi[t - h] for t in range(n)],
            [lo[t + h] if t < h else hi[t] for t in range(n)])


def _s5_modulated_rows(x_refs, mod_ref, b, load):
    ln = x_refs[0].shape[-1]
    mb = min(b, mod_ref.shape[0] - 1)
    halves = []
    for hf, x_ref in enumerate(x_refs):
        sc = 1.0 + mod_ref[mb, 1:2, hf * ln:(hf + 1) * ln]
        sh = mod_ref[mb, 0:1, hf * ln:(hf + 1) * ln]
        halves.append([load(x_ref, t) * sc + sh for t in range(S5_SUB)])
    return halves


def _s5_pack_kernel(xa_ref, xb_ref, mod_ref, o_ref, *, bsz, gp):
    n = S5_SUB

    def body(i, carry):
        for b in range(bsz):
            lo, hi = _chunk_butterfly(*_s5_modulated_rows(
                (xa_ref, xb_ref), mod_ref, b, lambda x_ref, t: x_ref[b, pl.ds(i * 8 * n + t, 8, stride=n), :]))
            for g in range(n):
                tile, gs = divmod(g, gp)
                rows = pl.ds(i * 64 + gs * bsz + b, 8, stride=8)
                o_ref[tile, 0, rows, :] = lo[g]
                o_ref[tile, 1, rows, :] = hi[g]
        return carry

    lax.fori_loop(0, xa_ref.shape[1] // (8 * n), body, 0)


def _s5_pack_cm_kernel(xa_ref, xb_ref, mod_ref, o_ref, *, bsz, gp):
    n = S5_SUB
    nrk = xa_ref.shape[1] // n

    def body(rk, carry):
        for b in range(bsz):
            lo, hi = _chunk_butterfly(*_s5_modulated_rows(
                (xa_ref, xb_ref), mod_ref, b, lambda x_ref, t: x_ref[b, rk * n + t]))
            for g in range(n):
                tile, gs = divmod(g, gp)
                rows = pl.ds(rk * 8 + gs * bsz + b, 8, stride=nrk * 8)
                o_ref[tile, 0, rows, :] = lo[g]
                o_ref[tile, 1, rows, :] = hi[g]
        return carry

    lax.fori_loop(0, nrk, body, 0)


def _s5_unpack_kernel(y_ref, o_ref, obuf, *, bsz, gp):
    n = S5_SUB

    def body(i, carry):
        for b in range(bsz):
            rows = lambda g: pl.ds(i * 64 + (g % gp) * bsz + b, 8, stride=8)
            lo, hi = _chunk_butterfly([y_ref[g // gp, 0, rows(g), :] for g in range(n)],
                                      [y_ref[g // gp, 1, rows(g), :] for g in range(n)])
            for t in range(n):
                tok = pl.ds(i * 8 * n + t, 8, stride=n)
                obuf[b, 0, tok, :] = lo[t]
                obuf[b, 1, tok, :] = hi[t]
        return carry

    lax.fori_loop(0, o_ref.shape[1] // (8 * n), body, 0)
    for b in range(bsz):
        o_ref[b] = jnp.concatenate([obuf[b, 0], obuf[b, 1]], axis=1).astype(o_ref.dtype)


def _s5_unpack_cm_kernel(y_ref, o_ref, *, bsz, gp):
    n = S5_SUB
    nrk = o_ref.shape[1] // n
    ln = y_ref.shape[-1]

    def body(rk, carry):
        for b in range(bsz):
            rows = lambda g: pl.ds(rk * 8 + (g % gp) * bsz + b, 8, stride=nrk * 8)
            lo, hi = _chunk_butterfly([y_ref[g // gp, 0, rows(g), :] for g in range(n)],
                                      [y_ref[g // gp, 1, rows(g), :] for g in range(n)])
            for t in range(n):
                o_ref[b, rk * n + t, :, 0:ln] = lo[t]
                o_ref[b, rk * n + t, :, ln:2 * ln] = hi[t]
        return carry

    lax.fori_loop(0, nrk, body, 0)


def _s5_pack(h, mod, gp, grid_rows=None):
    bsz, length, d = h.shape
    w = S5_SUB * S5_GROUP
    ln = S5_LANES
    assert S5_SUB == 16 and S5_GROUP == 16 and w == 2 * ln and d % w == 0
    tiles = S5_SUB // gp
    out_shape = jax.ShapeDtypeStruct((d // S5_GROUP // gp, 2, length // S5_SUB * 8, ln), F32)
    mod_spec = pl.BlockSpec((mod.shape[0], 8, w), lambda q, i: (0, 0, q))
    if grid_rows is None:
        tb = min(length, 512)
        return pl.pallas_call(
            functools.partial(_s5_pack_kernel, bsz=bsz, gp=gp),
            grid=(d // w, length // tb),
            in_specs=[pl.BlockSpec((bsz, tb, ln), lambda q, i: (0, i, 2 * q)),
                      pl.BlockSpec((bsz, tb, ln), lambda q, i: (0, i, 2 * q + 1)), mod_spec],
            out_specs=pl.BlockSpec((tiles, 2, tb // S5_SUB * 8, ln), lambda q, i: (q, 0, i, 0)),
            out_shape=out_shape,
            compiler_params=_cparams("parallel", "parallel"),
            name="s5_pack",
        )(h, h, mod)
    cols = length // grid_rows
    assert grid_rows % S5_SUB == 0 and cols % 8 == 0
    h4 = h.reshape(bsz, grid_rows, cols, d)
    return pl.pallas_call(
        functools.partial(_s5_pack_cm_kernel, bsz=bsz, gp=gp),
        grid=(d // w, cols // 8),
        in_specs=[pl.BlockSpec((bsz, grid_rows, 8, ln), lambda q, i: (0, 0, i, 2 * q)),
                  pl.BlockSpec((bsz, grid_rows, 8, ln), lambda q, i: (0, 0, i, 2 * q + 1)), mod_spec],
        out_specs=pl.BlockSpec((tiles, 2, grid_rows // 2 * 8, ln), lambda q, i: (q, 0, i, 0)),
        out_shape=out_shape,
        compiler_params=_cparams("parallel", "parallel"),
        name="s5_pack_cm",
    )(h4, h4, mod)


def _s5_unpack(y, bsz, gp, grid_rows=None):
    gt, hv, rows, ln = y.shape
    w = hv * ln
    length = rows // 8 * S5_SUB
    d = gt * gp * S5_GROUP
    tiles = S5_SUB // gp
    if grid_rows is None:
        tb = min(length, 512)
        return pl.pallas_call(
            functools.partial(_s5_unpack_kernel, bsz=bsz, gp=gp),
            grid=(d // w, length // tb),
            in_specs=[pl.BlockSpec((tiles, hv, tb // S5_SUB * 8, ln), lambda q, i: (q, 0, i, 0))],
            out_specs=pl.BlockSpec((bsz, tb, w), lambda q, i: (0, i, q)),
            out_shape=jax.ShapeDtypeStruct((bsz, length, d), BF16),
            scratch_shapes=[pltpu.VMEM((bsz, hv, tb, ln), F32)],
            compiler_params=_cparams("parallel", "parallel"),
            name="s5_unpack",
        )(y)
    cols = length // grid_rows
    out = pl.pallas_call(
        functools.partial(_s5_unpack_cm_kernel, bsz=bsz, gp=gp),
        grid=(d // w, cols // 8),
        in_specs=[pl.BlockSpec((tiles, hv, grid_rows // 2 * 8, ln), lambda q, i: (q, 0, i, 0))],
        out_specs=pl.BlockSpec((bsz, grid_rows, 8, w), lambda q, i: (0, 0, i, q)),
        out_shape=jax.ShapeDtypeStruct((bsz, grid_rows, cols, d), F32),
        compiler_params=_cparams("parallel", "parallel"),
        name="s5_unpack_cm",
    )(y)
    return out.reshape(bsz, length, d)


def _s5_pattern(v, gp, reps):
    d = v.shape[-1]
    gt = d // S5_GROUP // gp
    x = jnp.broadcast_to(v.reshape(gt, gp, 1, 1, S5_GROUP), (gt, gp, reps, S5_SUB, S5_GROUP))
    return x.reshape(gt, gp * reps, S5_SUB * S5_GROUP)


def _grid_transpose(x, rows, cols):
    bsz, _, ch = x.shape
    return x.reshape(bsz, rows, cols, ch).transpose(0, 2, 1, 3).reshape(bsz, rows * cols, ch)


def _s5_core(h_lat, h_ctx, mod_l, mod_c, params, col_major):
    lam_re, lam_im, log_dt, b_re, b_im, c_re, c_im, d_skip = params
    bsz, length, d = h_lat.shape
    assert 8 % bsz == 0 and S5_SUB % (8 // bsz) == 0
    gp = 8 // bsz
    gt = d // S5_GROUP // gp
    w = S5_SUB * S5_GROUP
    grid_rows = (length // GRID_W) if col_major else None
    xl = _s5_pack(h_lat, mod_l, gp, grid_rows)
    xc = _s5_pack(h_ctx, mod_c, gp)
    e_mat, m_mat, f_mat, a_mat = _s5_operators(lam_re, lam_im, log_dt, b_re, b_im, c_re, c_im)
    p2 = a_mat.shape[-1]
    cat = lambda m: m.reshape(gt, gp, m.shape[1], m.shape[2]).transpose(0, 2, 1, 3).reshape(gt, m.shape[1], -1)
    e_cat, m_cat, f_cat = (cat(m).astype(BF16) for m in (e_mat, m_mat, f_mat))
    a_pat = jnp.broadcast_to(a_mat.reshape(gt, gp, 1, 4, p2), (gt, gp, bsz, 4, p2)).transpose(0, 3, 1, 2, 4)
    a_pat = a_pat.reshape(gt, 4, 8, p2)
    d_pat = _s5_pattern(d_skip, gp, bsz)
    hv, rows_l, rows_c, ln = xl.shape[1], xl.shape[2], xc.shape[2], xl.shape[3]
    rc = 512
    tile = lambda *shape: pl.BlockSpec((1,) + shape, lambda g: (g,) + (0,) * len(shape))
    yc, yl = pl.pallas_call(
        functools.partial(_s5_kernel, bsz=bsz, gp=gp, rc=rc),
        grid=(gt,),
        in_specs=[tile(hv, rows_c, ln), tile(hv, rows_l, ln), tile(8, w), tile(4, 8, p2),
                  tile(w, gp * 4 * p2), tile(w, gp * w), tile(2 * p2, gp * w)],
        out_specs=[tile(hv, rows_c, ln), tile(hv, rows_l, ln)],
        out_shape=[jax.ShapeDtypeStruct((gt, hv, rows_c, ln), F32),
                   jax.ShapeDtypeStruct((gt, hv, rows_l, ln), F32)],
        scratch_shapes=[pltpu.VMEM((rows_c + rows_l, 4 * p2), F32)],
        compiler_params=_cparams("parallel"),
        name="s5_scan",
    )(xc, xl, d_pat, a_pat, e_cat, m_cat, f_cat)
    return _s5_unpack(yl, bsz, gp, grid_rows), _s5_unpack(yc, bsz, gp)


def _glu_ln_kernel(y_ref, h_ref, mod_ref, w_ref, b_ref, lng_ref, lnb_ref, o_ref, *, alpha):
    d = h_ref.shape[-1]
    z = jnp.dot(y_ref[0].astype(BF16), w_ref[...], preferred_element_type=F32) + b_ref[...]
    glu = z[:, :d] * _sigmoid(z[:, d:])
    g1 = 1.0 + mod_ref[0, 2:3, :]
    o_ref[0] = _ln(alpha * h_ref[0] + g1 * glu, lng_ref[...], lnb_ref[...])


def _glu_ln(y, h, mod, w_glu, b_glu, ln_g, ln_b, *, alpha, mod_row=None):
    bsz, length, d = h.shape
    tm = min(length, 512)
    mod_map = (lambda b, i: (b, 0, 0)) if mod_row is None else (lambda b, i: (mod_row, 0, 0))
    tok = lambda: pl.BlockSpec((1, tm, d), lambda b, i: (b, i, 0))
    return pl.pallas_call(
        functools.partial(_glu_ln_kernel, alpha=alpha),
        grid=(bsz, length // tm),
        in_specs=[tok(), tok(), pl.BlockSpec((1, 8, d), mod_map), _const_spec((d, 2 * d)), _const_spec((1, 2 * d)),
                  _const_spec((1, d)), _const_spec((1, d))],
        out_specs=tok(),
        out_shape=jax.ShapeDtypeStruct((bsz, length, d), F32),
        compiler_params=_cparams("parallel", "parallel"),
        name="glu_ln",
    )(y, h, mod, w_glu.astype(BF16), b_glu.reshape(1, 2 * d), ln_g.reshape(1, d), ln_b.reshape(1, d))


def _mod_map(mod_row):
    return (lambda b, i: (b, 0, 0)) if mod_row is None else (lambda b, i: (mod_row, 0, 0))


def _ml_up_kernel(h_ref, mod_ref, w_ref, xm_ref, z_ref):
    e = xm_ref.shape[-1]
    u = (h_ref[0] * (1.0 + mod_ref[0, 1:2, :]) + mod_ref[0, 0:1, :]).astype(BF16)
    y = jnp.dot(u, w_ref[...], preferred_element_type=F32)
    xm_ref[0] = y[:, :e]
    z_ref[0] = y[:, e:].astype(BF16)


def _ml_up(h, mod, w_up, *, mod_row=None):
    bsz, length, d = h.shape
    e = w_up.shape[1] // 2
    tm = min(length, 512)
    tok = lambda n: pl.BlockSpec((1, tm, n), lambda b, i: (b, i, 0))
    return pl.pallas_call(
        _ml_up_kernel,
        grid=(bsz, length // tm),
        in_specs=[tok(d), pl.BlockSpec((1, 8, d), _mod_map(mod_row)), _const_spec((d, 2 * e))],
        out_specs=[tok(e), tok(e)],
        out_shape=[jax.ShapeDtypeStruct((bsz, length, e), F32), jax.ShapeDtypeStruct((bsz, length, e), BF16)],
        compiler_params=_cparams("parallel", "parallel"),
        name="ml_up",
    )(h, mod, w_up.astype(BF16))


def _ml_qkv_kernel(prev_ref, x_ref, next_ref, cw_ref, cb_ref, w_ref, wg_ref, bg_ref,
                   xc_ref, q_ref, k_ref, v_ref, g_ref, xbuf, *, k_scale):
    i = pl.program_id(1)
    nt = pl.num_programs(1)
    tm, e = x_ref.shape[1], x_ref.shape[2]
    hp = prev_ref.shape[1]
    xbuf[0:hp, :] = jnp.where(i > 0, prev_ref[0], 0.0)
    xbuf[hp:hp + tm, :] = x_ref[0]
    xbuf[hp + tm:, :] = jnp.where(i < nt - 1, next_ref[0], 0.0)
    x = x_ref[0]
    conv = (cw_ref[0:1, :] * xbuf[hp - 1:hp - 1 + tm, :] + cw_ref[1:2, :] * x
            + cw_ref[2:3, :] * xbuf[hp + 1:hp + 1 + tm, :] + cb_ref[...])
    xc = _silu(conv).astype(BF16)
    xc_ref[0] = xc
    q = jnp.dot(xc, w_ref[0], preferred_element_type=F32).astype(BF16)
    k = (jnp.dot(xc, w_ref[1], preferred_element_type=F32) * k_scale).astype(BF16)
    v = jnp.dot(x.astype(BF16), w_ref[2], preferred_element_type=F32).astype(BF16)
    q_ref[0] = q
    k_ref[0] = k
    v_ref[0] = v
    g_ref[0] = (jnp.dot(q, wg_ref[0], preferred_element_type=F32) + jnp.dot(k, wg_ref[1], preferred_element_type=F32)
                + jnp.dot(v, wg_ref[2], preferred_element_type=F32) + bg_ref[...])


def _ml_qkv(xm, conv_w, conv_b, w_q, w_k, w_v, w_gates, b_gates):
    bsz, length, e = xm.shape
    tm = min(length, 256)
    hp = 8
    r = tm // hp
    nb = length // hp
    ng = w_gates.shape[-1]
    lanes = 128
    w3 = jnp.stack([w_q, w_k, w_v]).astype(BF16)
    wg = jnp.zeros((3, e, lanes), BF16).at[:, :, :ng].set(w_gates.astype(BF16))
    bg = jnp.zeros((1, lanes), F32).at[0, :ng].set(b_gates)
    cw = jnp.zeros((8, e), F32).at[:conv_w.shape[0]].set(conv_w)
    tok = lambda n: pl.BlockSpec((1, tm, n), lambda b, i: (b, i, 0))
    return pl.pallas_call(
        functools.partial(_ml_qkv_kernel, k_scale=(e // ML_HEADS) ** -0.5),
        grid=(bsz, length // tm),
        in_specs=[pl.BlockSpec((1, hp, e), lambda b, i: (b, jnp.maximum(i * r - 1, 0), 0)), tok(e),
                  pl.BlockSpec((1, hp, e), lambda b, i: (b, jnp.minimum((i + 1) * r, nb - 1), 0)),
                  _const_spec((8, e)), _const_spec((1, e)), _const_spec((3, e, e)), _const_spec((3, e, lanes)),
                  _const_spec((1, lanes))],
        out_specs=[tok(e), tok(e), tok(e), tok(e), tok(lanes)],
        out_shape=[jax.ShapeDtypeStruct((bsz, length, e), BF16)] * 4
                  + [jax.ShapeDtypeStruct((bsz, length, lanes), F32)],
        scratch_shapes=[pltpu.VMEM((tm + 2 * hp, e), F32)],
        compiler_params=_cparams("parallel", "parallel"),
        name="ml_qkv",
    )(xm, xm, xm, cw, conv_b.reshape(1, e), w3, wg, bg)


def _log_sigmoid(x):
    return jnp.minimum(x, 0.0) - jnp.log(1.0 + jnp.exp(-jnp.abs(x)))


def _ml_scan_kernel(*refs, nblk_c, heads):
    (qcf, kcf, vcf, gccf, grcf, qcb, kcb, vcb, gccb, grcb,
     qlf, klf, vlf, gclf, grlf, qlb, klb, vlb, gclb, grlb,
     hcf_ref, hcb_ref, hlf_ref, hlb_ref, c_sc, n_sc, m_sc) = refs
    i = pl.program_id(1)
    is_ctx = i < nblk_c
    t = qcf.shape[1]
    e = qcf.shape[2]
    dh = e // heads

    @pl.when(i == 0)
    def _():
        c_sc[...] = jnp.zeros_like(c_sc)
        n_sc[...] = jnp.zeros_like(n_sc)
        m_sc[...] = jnp.zeros_like(m_sc)

    row = lax.broadcasted_iota(jnp.int32, (t, t), 0)
    col = lax.broadcasted_iota(jnp.int32, (t, t), 1)
    pick = lambda c_ref, l_ref: jnp.where(is_ctx, c_ref[0], l_ref[0])
    outs = []
    for direction, blk in enumerate(((qcf, kcf, vcf, gccf, grcf, qlf, klf, vlf, gclf, grlf),
                                     (qcb, kcb, vcb, gccb, grcb, qlb, klb, vlb, gclb, grlb))):
        q_all, k_all, v_all, gc, gr = (pick(blk[j], blk[j + 5]) for j in range(5))
        mask = (col <= row) if direction == 0 else (col >= row)
        mask_t = (row <= col) if direction == 0 else (row >= col)
        h_heads = []
        for head in range(heads):
            r = direction * heads + head
            ci = direction * 2 * heads + head
            cf = ci + heads
            q = q_all[:, head * dh:(head + 1) * dh]
            k = k_all[:, head * dh:(head + 1) * dh]
            v = v_all[:, head * dh:(head + 1) * dh]
            ig_col, ig_row = gc[:, ci:ci + 1], gr[ci:ci + 1, :]
            lf_col, lf_row = _log_sigmoid(gc[:, cf:cf + 1]), _log_sigmoid(gr[cf:cf + 1, :])
            m_prev = m_sc[r, :, 0:1]
            b_col = jnp.sum(jnp.where(mask, jnp.broadcast_to(lf_row, (t, t)), 0.0), axis=1, keepdims=True)
            b_row = jnp.sum(jnp.where(mask_t, jnp.broadcast_to(lf_col, (t, t)), 0.0), axis=0, keepdims=True)
            b_tot = jnp.sum(lf_row, axis=1, keepdims=True)
            logd = jnp.where(mask, b_col - b_row + ig_row, -jnp.inf)
            m_inter = b_col + m_prev
            m_t = jnp.maximum(m_inter, jnp.max(logd, axis=1, keepdims=True))
            w_inter = jnp.exp(m_inter - m_t)
            qk = lax.dot_general(q, k, (((1,), (1,)), ((), ())), preferred_element_type=F32)
            s = jnp.exp(logd - m_t) * qk
            c_t = c_sc[r]
            n_row = n_sc[r]
            num = (w_inter * jnp.dot(q, c_t.astype(BF16), preferred_element_type=F32)
                   + jnp.dot(s.astype(BF16), v, preferred_element_type=F32))
            den = (w_inter * jnp.sum(q.astype(F32) * n_row, axis=1, keepdims=True)
                   + jnp.sum(s, axis=1, keepdims=True))
            h_heads.append(num / jnp.maximum(jnp.abs(den), jnp.exp(-m_t)))
            decay = b_tot - b_col + ig_col
            m_new = jnp.maximum(b_tot + m_prev, jnp.max(decay, axis=0, keepdims=True))
            w_prev = jnp.exp(b_tot + m_prev - m_new)
            w_r = jnp.exp(decay - m_new)
            wv = (w_r * v.astype(F32)).astype(BF16)
            c_sc[r] = w_prev * c_t + lax.dot_general(k, wv, (((0,), (0,)), ((), ())), preferred_element_type=F32)
            n_sc[r] = w_prev * n_row + jnp.sum(w_r * k.astype(F32), axis=0, keepdims=True)
            m_sc[r] = jnp.broadcast_to(m_new, m_sc.shape[1:])
        outs.append(jnp.concatenate(h_heads, axis=1))

    @pl.when(is_ctx)
    def _():
        hcf_ref[0] = outs[0]
        hcb_ref[0] = outs[1]

    @pl.when(jnp.logical_not(is_ctx))
    def _():
        hlf_ref[0] = outs[0]
        hlb_ref[0] = outs[1]


def _ml_scan(qkv_c, g_c, qkv_l, g_l):
    bsz, len_c, e = qkv_c[0].shape
    len_l = qkv_l[0].shape[1]
    t = SCAN_BLOCK
    nc, nl = len_c // t, len_l // t
    lanes = g_c.shape[-1]
    rows = 16
    grow = lambda g: jnp.swapaxes(g[:, :, :rows], 1, 2)
    cf = lambda i: jnp.minimum(i, nc - 1)
    cb = lambda i: jnp.maximum(nc - 1 - i, 0)
    lf = lambda i: jnp.maximum(i - nc, 0)
    lb = lambda i: jnp.minimum(nl - 1 - (i - nc), nl - 1)
    def specs(idx):
        tok = pl.BlockSpec((1, t, e), lambda b, i: (b, idx(i), 0))
        return [tok, tok, tok, pl.BlockSpec((1, t, lanes), lambda b, i: (b, idx(i), 0)),
                pl.BlockSpec((1, rows, t), lambda b, i: (b, 0, idx(i)))]
    args_c = list(qkv_c) + [g_c, grow(g_c)]
    args_l = list(qkv_l) + [g_l, grow(g_l)]
    out = lambda idx: pl.BlockSpec((1, t, e), lambda b, i: (b, idx(i), 0))
    heads = ML_HEADS
    dh = e // heads
    return pl.pallas_call(
        functools.partial(_ml_scan_kernel, nblk_c=nc, heads=heads),
        grid=(bsz, nc + nl),
        in_specs=specs(cf) + specs(cb) + specs(lf) + specs(lb),
        out_specs=[out(cf), out(cb), out(lf), out(lb)],
        out_shape=[jax.ShapeDtypeStruct((bsz, len_c, e), F32)] * 2 + [jax.ShapeDtypeStruct((bsz, len_l, e), F32)] * 2,
        scratch_shapes=[pltpu.VMEM((2 * heads, dh, dh), F32), pltpu.VMEM((2 * heads, 1, dh), F32),
                        pltpu.VMEM((2 * heads, 1, 128), F32)],
        compiler_params=_cparams("parallel", "arbitrary"),
        name="ml_scan",
    )(*(args_c + args_c + args_l + args_l))


def _ml_out_kernel(hf_ref, hb_ref, xc_ref, z_ref, h_ref, mod_ref, gn_ref, sk_ref, w_ref, lng_ref, lnb_ref, o_ref,
                   *, alpha, heads):
    e = hf_ref.shape[-1]
    dh = e // heads
    hh = hf_ref[0] + hb_ref[0]
    parts = []
    for head in range(heads):
        x = hh[:, head * dh:(head + 1) * dh]
        mu = jnp.mean(x, axis=-1, keepdims=True)
        xc = x - mu
        var = jnp.mean(xc * xc, axis=-1, keepdims=True)
        parts.append(xc * lax.rsqrt(var + LN_EPS))
    hn = jnp.concatenate(parts, axis=1) * gn_ref[...] + sk_ref[...] * xc_ref[0].astype(F32)
    a = (hn * _silu(z_ref[0].astype(F32))).astype(BF16)
    y = jnp.dot(a, w_ref[...], preferred_element_type=F32)
    g1 = 1.0 + mod_ref[0, 2:3, :]
    o_ref[0] = _ln(alpha * h_ref[0] + g1 * y, lng_ref[...], lnb_ref[...])


def _ml_out(hf, hb, xc, z, h, mod, gn_g, skip, w_down, ln_g, ln_b, *, alpha, mod_row=None):
    bsz, length, d = h.shape
    e = hf.shape[-1]
    tm = min(length, 256)
    tok = lambda n: pl.BlockSpec((1, tm, n), lambda b, i: (b, i, 0))
    return pl.pallas_call(
        functools.partial(_ml_out_kernel, alpha=alpha, heads=ML_HEADS),
        grid=(bsz, length // tm),
        in_specs=[tok(e), tok(e), tok(e), tok(e), tok(d), pl.BlockSpec((1, 8, d), _mod_map(mod_row)),
                  _const_spec((1, e)), _const_spec((1, e)), _const_spec((e, d)), _const_spec((1, d)),
                  _const_spec((1, d))],
        out_specs=tok(d),
        out_shape=jax.ShapeDtypeStruct((bsz, length, d), F32),
        compiler_params=_cparams("parallel", "parallel"),
        name="ml_out",
    )(hf, hb, xc, z, h, mod, gn_g.reshape(1, e), skip.reshape(1, e), w_down.astype(BF16),
      ln_g.reshape(1, d), ln_b.reshape(1, d))


def _mlstm_layer(h_lat, h_ctx, mod_l, mod_c, p, ln_g, ln_b, *, alpha, need_ctx):
    (w_up, conv_w, conv_b, w_q, w_k, w_v, w_gates, b_gates, gn_g, skip, w_down) = p
    xm_l, z_l = _ml_up(h_lat, mod_l, w_up)
    xm_c, z_c = _ml_up(h_ctx, mod_c, w_up, mod_row=0)
    xc_l, q_l, k_l, v_l, g_l = _ml_qkv(xm_l, conv_w, conv_b, w_q, w_k, w_v, w_gates, b_gates)
    xc_c, q_c, k_c, v_c, g_c = _ml_qkv(xm_c, conv_w, conv_b, w_q, w_k, w_v, w_gates, b_gates)
    hcf, hcb, hlf, hlb = _ml_scan((q_c, k_c, v_c), g_c, (q_l, k_l, v_l), g_l)
    out_l = _ml_out(hlf, hlb, xc_l, z_l, h_lat, mod_l, gn_g, skip, w_down, ln_g, ln_b, alpha=alpha)
    out_c = None
    if need_ctx:
        out_c = _ml_out(hcf, hcb, xc_c, z_c, h_ctx, mod_c, gn_g, skip, w_down, ln_g, ln_b, alpha=alpha, mod_row=0)
    return out_l, out_c


def _cv_in_kernel(h_ref, mod_ref, w_ref, b_ref, o_ref):
    d = h_ref.shape[-1]
    u = (h_ref[0] * (1.0 + mod_ref[0, 1:2, :]) + mod_ref[0, 0:1, :]).astype(BF16)
    y = jnp.dot(u, w_ref[...], preferred_element_type=F32) + b_ref[...]
    o_ref[0] = y[:, :d] * _sigmoid(y[:, d:])


def _cv_in(h, mod, w_in, b_in, *, mod_row=None):
    bsz, length, d = h.shape
    tm = min(length, 512)
    tok = lambda: pl.BlockSpec((1, tm, d), lambda b, i: (b, i, 0))
    return pl.pallas_call(
        _cv_in_kernel,
        grid=(bsz, length // tm),
        in_specs=[tok(), pl.BlockSpec((1, 8, d), _mod_map(mod_row)), _const_spec((d, 2 * d)),
                  _const_spec((1, 2 * d))],
        out_specs=tok(),
        out_shape=jax.ShapeDtypeStruct((bsz, length, d), F32),
        compiler_params=_cparams("parallel", "parallel"),
        name="cv_in",
    )(h, mod, w_in.astype(BF16), b_in.reshape(1, 2 * d))


def _cv_out_kernel(prev_ref, x_ref, next_ref, h_ref, mod_ref, dw_ref, db_ref, cg_ref, cb_ref, w_ref, b_ref,
                   lng_ref, lnb_ref, o_ref, sbuf, cbuf, *, alpha, taps):
    i = pl.program_id(1)
    nt = pl.num_programs(1)
    tm = x_ref.shape[1]
    hp = prev_ref.shape[1]
    rows = tm + 2 * hp
    sbuf[0, 0:hp, :] = jnp.where(i > 0, prev_ref[0], 0.0)
    sbuf[0, hp:hp + tm, :] = x_ref[0]
    sbuf[0, hp + tm:, :] = jnp.where(i < nt - 1, next_ref[0], 0.0)
    for s in range(1, 8):
        sbuf[s, 0:rows - 8, :] = sbuf[0, s:s + rows - 8, :]
    left = (taps - 1) // 2
    rb = 16

    def conv_rows(blk, carry):
        r0 = pl.multiple_of(blk * rb, rb)
        acc = None
        for k in range(taps):
            off = hp + k - left
            win = sbuf[off % 8, pl.ds(r0 + (off // 8) * 8, rb), :].reshape(rb // 8, 8, -1)
            term = dw_ref[k] * win
            acc = term if acc is None else acc + term
        cbuf[pl.ds(r0, rb), :] = acc.reshape(rb, -1)
        return carry

    lax.fori_loop(0, tm // rb, conv_rows, 0)
    c = _silu(_ln(cbuf[...] + db_ref[...], cg_ref[...], cb_ref[...])).astype(BF16)
    y = jnp.dot(c, w_ref[...], preferred_element_type=F32) + b_ref[...]
    g1 = 1.0 + mod_ref[0, 2:3, :]
    o_ref[0] = _ln(alpha * h_ref[0] + g1 * y, lng_ref[...], lnb_ref[...])


def _cv_out(hmid, h, mod, dw_w, dw_b, cln_g, cln_b, w_out, b_out, ln_g, ln_b, *, alpha, mod_row=None):
    bsz, length, d = h.shape
    taps = dw_w.shape[0]
    tm = min(length, 256)
    hp = 16
    assert (taps - 1) // 2 <= hp and taps // 2 <= hp
    r = tm // hp
    nb = length // hp
    dw = jnp.broadcast_to(dw_w[:, None, :], (taps, 8, d))
    tok = lambda: pl.BlockSpec((1, tm, d), lambda b, i: (b, i, 0))
    vec = lambda: _const_spec((1, d))
    return pl.pallas_call(
        functools.partial(_cv_out_kernel, alpha=alpha, taps=taps),
        grid=(bsz, length // tm),
        in_specs=[pl.BlockSpec((1, hp, d), lambda b, i: (b, jnp.maximum(i * r - 1, 0), 0)), tok(),
                  pl.BlockSpec((1, hp, d), lambda b, i: (b, jnp.minimum((i + 1) * r, nb - 1), 0)),
                  tok(), pl.BlockSpec((1, 8, d), _mod_map(mod_row)), _const_spec((taps, 8, d)), vec(), vec(), vec(),
                  _const_spec((d, d)), vec(), vec(), vec()],
        out_specs=tok(),
        out_shape=jax.ShapeDtypeStruct((bsz, length, d), F32),
        scratch_shapes=[pltpu.VMEM((8, tm + 2 * hp, d), F32), pltpu.VMEM((tm, d), F32)],
        compiler_params=_cparams("parallel", "parallel"),
        name="cv_out",
    )(hmid, hmid, hmid, h, mod, dw, dw_b.reshape(1, d), cln_g.reshape(1, d), cln_b.reshape(1, d),
      w_out.astype(BF16), b_out.reshape(1, d), ln_g.reshape(1, d), ln_b.reshape(1, d))


def _jln(x, g, b):
    mu = x.mean(-1, keepdims=True)
    var = jnp.square(x - mu).mean(-1, keepdims=True)
    return (x - mu) * lax.rsqrt(var + LN_EPS) * g + b


def _j_dwconv1d(x, w, b):
    k = w.shape[0]
    y = lax.conv_general_dilated(x, w[:, None, :], (1,), [((k - 1) // 2, k // 2)],
                                 dimension_numbers=('NWC', 'WIO', 'NWC'), feature_group_count=x.shape[-1])
    return y + b


def _j_s5_discretise(lam_re, lam_im, log_dt, b_re, b_im):
    lre = jnp.minimum(lam_re, -1e-4)
    lim = lam_im
    dt = jnp.exp(log_dt)[:, None]
    mag = jnp.exp(lre * dt)
    lb_re, lb_im = mag * jnp.cos(lim * dt), mag * jnp.sin(lim * dt)
    nr, ni = lb_re - 1.0, lb_im
    den = lre * lre + lim * lim
    cr = (nr * lre + ni * lim) / den
    ci = (ni * lre - nr * lim) / den
    bb_re = cr[..., None] * b_re - ci[..., None] * b_im
    bb_im = cr[..., None] * b_im + ci[..., None] * b_re
    return lb_re, lb_im, bb_re, bb_im


def _j_combine(e1, e2):
    a1r, a1i, b1r, b1i = e1
    a2r, a2i, b2r, b2i = e2
    return (a2r * a1r - a2i * a1i, a2r * a1i + a2i * a1r,
            a2r * b1r - a2i * b1i + b2r, a2r * b1i + a2i * b1r + b2i)


def _j_s5_scan(u, lb_re, lb_im, bb_re, bb_im, c_re, c_im, s0):
    bsz, length, d = u.shape
    groups = d // S5_GROUP
    nblk = length // 128
    ub = jnp.moveaxis(u.reshape(bsz, nblk, 128, groups, S5_GROUP), 1, 0)

    def step(carry, u_blk):
        sr, si = carry
        bu_re = jnp.einsum('btgs,gps->btgp', u_blk, bb_re)
        bu_im = jnp.einsum('btgs,gps->btgp', u_blk, bb_im)
        a_re = jnp.broadcast_to(lb_re, bu_re.shape)
        a_im = jnp.broadcast_to(lb_im, bu_im.shape)
        pa_re, pa_im, pb_re, pb_im = lax.associative_scan(_j_combine, (a_re, a_im, bu_re, bu_im), axis=1)
        st_re = pa_re * sr[:, None] - pa_im * si[:, None] + pb_re
        st_im = pa_re * si[:, None] + pa_im * sr[:, None] + pb_im
        y = jnp.einsum('gsp,btgp->btgs', c_re, st_re) - jnp.einsum('gsp,btgp->btgs', c_im, st_im)
        return (st_re[:, -1], st_im[:, -1]), y.reshape(bsz, 128, d)

    s_end, ys = lax.scan(step, s0, ub)
    return jnp.moveaxis(ys, 0, 1).reshape(bsz, length, d), s_end


def _j_s5_mixer(ul, uc, lam_re, lam_im, log_dt, b_re, b_im, c_re, c_im, d_skip, w_glu, b_glu):
    y_lat, y_ctx = d_skip * ul, d_skip * uc
    bsz = ul.shape[0]
    groups, p = lam_re.shape[1], lam_re.shape[2]
    for direction in range(2):
        lb_re, lb_im, bb_re, bb_im = _j_s5_discretise(lam_re[direction], lam_im[direction], log_dt[direction],
                                                      b_re[direction], b_im[direction])
        cr, ci = c_re[direction], c_im[direction]
        zero = jnp.zeros((bsz, groups, p), F32)
        rev = (lambda a: a[:, ::-1]) if direction == 1 else (lambda a: a)
        yc, s_ctx = _j_s5_scan(rev(uc), lb_re, lb_im, bb_re, bb_im, cr, ci, (zero, zero))
        yl, _ = _j_s5_scan(rev(ul), lb_re, lb_im, bb_re, bb_im, cr, ci, s_ctx)
        y_ctx = y_ctx + rev(yc)
        y_lat = y_lat + rev(yl)

    def glu(y):
        z = jax.nn.gelu(y) @ w_glu + b_glu
        val, gate = jnp.split(z, 2, axis=-1)
        return val * jax.nn.sigmoid(gate)

    return glu(y_lat), glu(y_ctx)


def _j_mlstm_scan(q, k, v, ig, lf, state):
    bsz, nh, length, dh = q.shape
    nblk = length // SCAN_BLOCK
    blocks = lambda a: jnp.moveaxis(a.reshape(bsz, nh, nblk, SCAN_BLOCK, *a.shape[3:]), 2, 0)
    tri = jnp.tril(jnp.ones((SCAN_BLOCK, SCAN_BLOCK), bool))

    def step(carry, blk):
        C, n, m = carry
        qb, kb, vb, ib, fb = blk
        b = jnp.cumsum(fb, axis=-1)
        logd = jnp.where(tri, b[..., :, None] - b[..., None, :] + ib[..., None, :], -jnp.inf)
        m_inter = b + m[..., None]
        m_t = jnp.maximum(m_inter, logd.max(-1))
        w_inter = jnp.exp(m_inter - m_t)
        s = jnp.exp(logd - m_t[..., None]) * jnp.einsum('bhtd,bhsd->bhts', qb, kb)
        num = w_inter[..., None] * jnp.einsum('bhed,bhtd->bhte', C, qb) + jnp.einsum('bhts,bhse->bhte', s, vb)
        den = w_inter * jnp.einsum('bhd,bhtd->bht', n, qb) + s.sum(-1)
        h = num / jnp.maximum(jnp.abs(den), jnp.exp(-m_t))[..., None]
        decay = b[..., -1:] - b + ib
        m_new = jnp.maximum(b[..., -1] + m, decay.max(-1))
        w_prev = jnp.exp(b[..., -1] + m - m_new)
        w_r = jnp.exp(decay - m_new[..., None])
        C = w_prev[..., None, None] * C + jnp.einsum('bhse,bhsd->bhed', w_r[..., None] * vb, kb)
        n = w_prev[..., None] * n + jnp.einsum('bhs,bhsd->bhd', w_r, kb)
        return (C, n, m_new), h

    state, hs = lax.scan(step, state, tuple(blocks(a) for a in (q, k, v, ig, lf)))
    return jnp.moveaxis(hs, 0, 2).reshape(bsz, nh, length, dh), state


def _j_mlstm_project(u, w_up, conv_w, conv_b, w_q, w_k, w_v, w_gates, b_gates):
    bsz, length, _ = u.shape
    e = w_q.shape[0]
    dh = e // ML_HEADS
    xm, z = jnp.split(u @ w_up, 2, axis=-1)
    xc = jax.nn.silu(_j_dwconv1d(xm, conv_w, conv_b))
    q = xc @ w_q
    k = (xc @ w_k) * dh ** -0.5
    v = xm @ w_v
    g = q @ w_gates[0] + k @ w_gates[1] + v @ w_gates[2] + b_gates
    heads = lambda a: a.reshape(bsz, length, ML_HEADS, dh).transpose(0, 2, 1, 3)
    return heads(q), heads(k), heads(v), g.reshape(bsz, length, 2, 2, ML_HEADS), xc, z


def _j_mlstm_out(h, xc, z, gn_g, skip, w_down):
    bsz, nh, length, dh = h.shape
    mu = h.mean(-1, keepdims=True)
    var = jnp.square(h - mu).mean(-1, keepdims=True)
    hn = ((h - mu) * lax.rsqrt(var + LN_EPS)).transpose(0, 2, 1, 3).reshape(bsz, length, nh * dh)
    hn = hn * gn_g + skip * xc
    return (hn * jax.nn.silu(z)) @ w_down


def _j_mlstm_mixer(u_lat, u_ctx, w_up, conv_w, conv_b, w_q, w_k, w_v, w_gates, b_gates, gn_g, skip, w_down):
    lat = _j_mlstm_project(u_lat, w_up, conv_w, conv_b, w_q, w_k, w_v, w_gates, b_gates)
    cxt = _j_mlstm_project(u_ctx, w_up, conv_w, conv_b, w_q, w_k, w_v, w_gates, b_gates)
    bsz = u_lat.shape[0]
    dh = w_q.shape[0] // ML_HEADS
    h_lat = jnp.zeros(lat[0].shape, F32)
    h_ctx = jnp.zeros(cxt[0].shape, F32)
    for direction in range(2):
        rev = (lambda a: jnp.flip(a, axis=2)) if direction == 1 else (lambda a: a)

        def prep(p):
            q, k, v, g = p[:4]
            ig = jnp.moveaxis(g[:, :, direction, 0], 1, 2)
            lf = jax.nn.log_sigmoid(jnp.moveaxis(g[:, :, direction, 1], 1, 2))
            return tuple(rev(a) for a in (q, k, v, ig, lf))

        st0 = (jnp.zeros((bsz, ML_HEADS, dh, dh), F32), jnp.zeros((bsz, ML_HEADS, dh), F32),
               jnp.zeros((bsz, ML_HEADS), F32))
        hc, st_ctx = _j_mlstm_scan(*prep(cxt), st0)
        hl, _ = _j_mlstm_scan(*prep(lat), st_ctx)
        h_ctx = h_ctx + rev(hc)
        h_lat = h_lat + rev(hl)
    return (_j_mlstm_out(h_lat, lat[4], lat[5], gn_g, skip, w_down),
            _j_mlstm_out(h_ctx, cxt[4], cxt[5], gn_g, skip, w_down))


def _j_conformer(u, w_in, b_in, dw_w, dw_b, ln_g, ln_b, w_out, b_out):
    a, gate = jnp.split(u @ w_in + b_in, 2, axis=-1)
    hmid = a * jax.nn.sigmoid(gate)
    hmid = jax.nn.silu(_jln(_j_dwconv1d(hmid, dw_w, dw_b), ln_g, ln_b))
    return hmid @ w_out + b_out


def _grid_transpose(x, rows, cols):
    bsz, _, ch = x.shape
    return x.reshape(bsz, rows, cols, ch).transpose(0, 2, 1, 3).reshape(bsz, rows * cols, ch)


def kernel(x, c, ctx, c_ctx, mod_w, mod_b, post_ln_g, post_ln_b, ffn_w_gate, ffn_w_up, ffn_conv_w, ffn_conv_b, ffn_w_down, s5_lambda_re, s5_lambda_im, s5_log_dt, s5_b_re, s5_b_im, s5_c_re, s5_c_im, s5_d, s5_w_glu, s5_b_glu, ml_w_up, ml_conv_w, ml_conv_b, ml_w_q, ml_w_k, ml_w_v, ml_w_gates, ml_b_gates, ml_gn_g, ml_skip, ml_w_down, cv_w_in, cv_b_in, cv_dw_w, cv_dw_b, cv_ln_g, cv_ln_b, cv_w_out, cv_b_out):
    bsz, length, d = x.shape
    depth = mod_w.shape[0]
    ctx_len = ctx.shape[1]
    rows = length // GRID_W
    alpha = (2 * depth) ** 0.25

    modv = _modulation(c, c_ctx, mod_w, mod_b)
    pad2 = jnp.zeros((2, d), F32)
    h_lat, h_ctx = x, ctx
    for i in range(depth):
        kind, occ = i % N_MIXERS, i // N_MIXERS
        last = i == depth - 1
        mod_l = jnp.concatenate([modv[i, :bsz].reshape(bsz, 6, d), jnp.zeros((bsz, 2, d), F32)], axis=1)
        mod_c = jnp.concatenate([modv[i, bsz].reshape(6, d), pad2], axis=0)[None]
        col_major = (kind != 2) and (occ % 2 == 1)
        ln1 = (post_ln_g[i, 0], post_ln_b[i, 0])
        ffn = (ffn_w_gate[i], ffn_w_up[i], ffn_conv_w[i], ffn_conv_b[i], ffn_w_down[i],
               post_ln_g[i, 1], post_ln_b[i, 1])
        if kind == 0:
            s5 = (s5_lambda_re[occ], s5_lambda_im[occ], s5_log_dt[occ], s5_b_re[occ], s5_b_im[occ],
                  s5_c_re[occ], s5_c_im[occ], s5_d[occ])
            y_lat, y_ctx = _s5_core(h_lat, h_ctx, mod_l, mod_c, s5, col_major)
            h_lat = _glu_ln(y_lat, h_lat, mod_l, s5_w_glu[occ], s5_b_glu[occ], *ln1, alpha=alpha)
            if not last:
                h_ctx = _glu_ln(y_ctx, h_ctx, mod_c, s5_w_glu[occ], s5_b_glu[occ], *ln1, alpha=alpha, mod_row=0)
        elif kind == 1:
            ml = (ml_w_up[occ], ml_conv_w[occ], ml_conv_b[occ], ml_w_q[occ], ml_w_k[occ], ml_w_v[occ],
                  ml_w_gates[occ], ml_b_gates[occ], ml_gn_g[occ], ml_skip[occ], ml_w_down[occ])
            h_in = _grid_transpose(h_lat, rows, GRID_W) if col_major else h_lat
            h_lat, h_ctx = _mlstm_layer(h_in, h_ctx, mod_l, mod_c, ml, *ln1, alpha=alpha, need_ctx=not last)
            if col_major:
                h_lat = _grid_transpose(h_lat, GRID_W, rows)
        else:
            hm_l = _cv_in(h_lat, mod_l, cv_w_in[occ], cv_b_in[occ])
            cv = (cv_dw_w[occ], cv_dw_b[occ], cv_ln_g[occ], cv_ln_b[occ], cv_w_out[occ], cv_b_out[occ])
            if not last:
                hm_c = _cv_in(h_ctx, mod_c, cv_w_in[occ], cv_b_in[occ], mod_row=0)
                h_ctx = _cv_out(hm_c, h_ctx, mod_c, *cv, *ln1, alpha=alpha, mod_row=0)
            h_lat = _cv_out(hm_l, h_lat, mod_l, *cv, *ln1, alpha=alpha)
        h_lat = _conv_ffn_ln(h_lat, mod_l, *ffn, width=GRID_W, alpha=alpha)
        if not last:
            h_ctx = _conv_ffn_ln(h_ctx, mod_c, *ffn, width=ctx_len, alpha=alpha, mod_row=0)
    return h_lat
```

```python
import functools
import math

import jax
import jax.numpy as jnp
import numpy as np
from jax import lax
from jax.experimental import pallas as pl
from jax.experimental.pallas import tpu as pltpu

F32 = jnp.float32
BF16 = jnp.bfloat16

GRID_W = 64
SCAN_BLOCK = 128
S5_GROUP = 16
S5_SUB = 16
S5_LANES = 128
ML_HEADS = 4
FFN_TILE = 1024
N_MIXERS = 3
LN_EPS = 1e-5
VMEM_LIMIT = 56 * 1024 * 1024


def _cparams(*sem):
    return pltpu.CompilerParams(dimension_semantics=sem, vmem_limit_bytes=VMEM_LIMIT)


def _const_spec(shape):
    nd = len(shape)
    return pl.BlockSpec(shape, lambda *_: (0,) * nd, pipeline_mode=pl.Buffered(1))


def _ln(x, g, b):
    mu = jnp.mean(x, axis=-1, keepdims=True)
    xc = x - mu
    var = jnp.mean(xc * xc, axis=-1, keepdims=True)
    return xc * lax.rsqrt(var + LN_EPS) * g + b


def _sigmoid(x):
    return 1.0 / (1.0 + jnp.exp(-x))


def _silu(x):
    return x * _sigmoid(x)


def _gelu(x):
    return 0.5 * x * (1.0 + jnp.tanh(math.sqrt(2.0 / math.pi) * (x + 0.044715 * (x * x * x))))


def _bdot(a, b):
    return jnp.dot(a.astype(BF16), b.astype(BF16), preferred_element_type=F32)


def _mod_kernel(c_ref, w_ref, b_ref, o_ref):
    o_ref[0] = _bdot(_silu(c_ref[...]), w_ref[0]) + b_ref[0]


def _modulation(c, c_ctx, mod_w, mod_b):
    depth, d, d6 = mod_w.shape
    bsz = c.shape[0]
    rows = 8
    cs = jnp.zeros((rows, d), F32).at[:bsz].set(c).at[bsz].set(c_ctx)
    tn = d6 // 6
    return pl.pallas_call(
        _mod_kernel,
        grid=(depth, d6 // tn),
        in_specs=[pl.BlockSpec((rows, d), lambda i, j: (0, 0)),
                  pl.BlockSpec((1, d, tn), lambda i, j: (i, 0, j)),
                  pl.BlockSpec((1, 1, tn), lambda i, j: (i, 0, j))],
        out_specs=pl.BlockSpec((1, rows, tn), lambda i, j: (i, 0, j)),
        out_shape=jax.ShapeDtypeStruct((depth, rows, d6), F32),
        compiler_params=_cparams("parallel", "parallel"),
        name="modulation",
    )(cs, mod_w, mod_b.reshape(depth, 1, d6))


def _ffn_kernel(*refs, width, tm, n_chunks, alpha, halo):
    if halo:
        top_ref, x_ref, bot_ref = refs[:3]
        refs = refs[3:]
    else:
        x_ref = refs[0]
        refs = refs[1:]
    mod_ref, wg_ref, wu_ref, cw_ref, cb_ref, wd_ref, lng_ref, lnb_ref, o_ref, xbuf, gbuf, ubuf, abuf, acc = refs
    i = pl.program_id(1)
    nt = pl.num_programs(1)
    pad = 8
    hw = width if halo else 0
    fc = gbuf.shape[-1]

    sh2 = mod_ref[0, 3:4, :]
    sc2 = 1.0 + mod_ref[0, 4:5, :]
    g2 = 1.0 + mod_ref[0, 5:6, :]
    x = x_ref[0]
    xbuf[hw:hw + tm, :] = (x * sc2 + sh2).astype(BF16)
    if halo:
        xbuf[0:hw, :] = (top_ref[0] * sc2 + sh2).astype(BF16)
        xbuf[hw + tm:, :] = (bot_ref[0] * sc2 + sh2).astype(BF16)
    for slot in range(2):
        gbuf[slot, 0:pad, :] = jnp.zeros((pad, fc), F32)
        gbuf[slot, pad + tm + 2 * hw:, :] = jnp.zeros((pad, fc), F32)
    keep_top = (i > 0).astype(F32)
    keep_bot = (i < nt - 1).astype(F32)

    col = lax.broadcasted_iota(jnp.int32, (tm, fc), 0) & (width - 1)
    not_first = col > 0
    not_last = col < width - 1

    def project(j, slot):
        g = jnp.dot(xbuf[...], wg_ref[j], preferred_element_type=F32)
        if halo:
            gbuf[slot, pad:pad + hw, :] = g[0:hw] * keep_top
            gbuf[slot, pad + hw:pad + hw + tm, :] = g[hw:hw + tm]
            gbuf[slot, pad + hw + tm:pad + 2 * hw + tm, :] = g[hw + tm:] * keep_bot
        else:
            gbuf[slot, pad:pad + tm, :] = g
        ubuf[slot] = jnp.dot(xbuf[hw:hw + tm, :], wu_ref[j], preferred_element_type=F32)

    def activate(j, slot):
        def taps(dc):
            t = None
            for dr in ((-1, 0, 1) if halo else (0,)):
                start = pad + hw + dr * width + dc
                w = cw_ref[j, 3 * (dr + 1) + dc + 1:3 * (dr + 1) + dc + 2, :]
                term = gbuf[slot, pl.ds(start, tm), :] * w
                t = term if t is None else t + term
            return t

        gate = (taps(0) + jnp.where(not_first, taps(-1), 0.0) + jnp.where(not_last, taps(1), 0.0)
                + cb_ref[j])
        abuf[slot] = (_gelu(gate) * ubuf[slot]).astype(BF16)

    def contract(j, slot):
        return jnp.dot(abuf[slot], wd_ref[j], preferred_element_type=F32)

    project(0, 0)
    if n_chunks > 1:
        project(1, 1)
    activate(0, 0)

    def step(j, slot):
        acc[...] += contract(j - 1, 1 - slot)
        project(j + 1, 1 - slot)
        activate(j, slot)

    def body(p, carry):
        step(1 + 2 * p, 1)
        step(2 + 2 * p, 0)
        return carry

    acc[...] = jnp.zeros_like(acc)
    pairs = max(n_chunks - 2, 0) // 2
    if pairs:
        lax.fori_loop(0, pairs, body, 0)
    if max(n_chunks - 2, 0) % 2:
        step(1 + 2 * pairs, 1)
    if n_chunks > 1:
        last = n_chunks - 1
        acc[...] += contract(last - 1, 1 - (last & 1))
        activate(last, last & 1)
    y = acc[...] + contract(n_chunks - 1, (n_chunks - 1) & 1)
    o_ref[0] = _ln(alpha * x + g2 * y, lng_ref[...], lnb_ref[...])


def _ffn_chunk(f):
    for fc in (256, 128):
        if f % fc == 0:
            return fc
    raise ValueError(f"ffn hidden size {f} is not a multiple of 128")


def _conv_ffn_ln(h, mod, w_gate, w_up, conv_w, conv_b, w_down, ln_g, ln_b, *, width, alpha, mod_row=None):
    bsz, length, d = h.shape
    f = w_gate.shape[1]
    fc = _ffn_chunk(f)
    nf = f // fc
    rows = length // width
    halo = rows > 1
    tm = min(length, FFN_TILE) if halo else length
    assert length % tm == 0 and tm % width == 0 and width & (width - 1) == 0
    nt = length // tm
    r = tm // width
    wg = w_gate.astype(BF16).reshape(d, nf, fc).transpose(1, 0, 2)
    wu = w_up.astype(BF16).reshape(d, nf, fc).transpose(1, 0, 2)
    wd = w_down.astype(BF16).reshape(nf, fc, d)
    cw = conv_w.reshape(9, nf, fc).transpose(1, 0, 2)
    cb = conv_b.reshape(nf, 1, fc)
    mod_map = (lambda b, i: (b, 0, 0)) if mod_row is None else (lambda b, i: (mod_row, 0, 0))
    x_spec = pl.BlockSpec((1, tm, d), lambda b, i: (b, i, 0))
    in_specs, args = [x_spec], [h]
    if halo:
        nrow = length // width
        in_specs = [pl.BlockSpec((1, width, d), lambda b, i: (b, jnp.maximum(i * r - 1, 0), 0)), x_spec,
                    pl.BlockSpec((1, width, d), lambda b, i: (b, jnp.minimum((i + 1) * r, nrow - 1), 0))]
        args = [h, h, h]
    in_specs += [pl.BlockSpec((1, 8, d), mod_map), _const_spec(wg.shape), _const_spec(wu.shape),
                 _const_spec(cw.shape), _const_spec(cb.shape), _const_spec(wd.shape),
                 _const_spec((1, d)), _const_spec((1, d))]
    args += [mod, wg, wu, cw, cb, wd, ln_g.reshape(1, d), ln_b.reshape(1, d)]
    hw = width if halo else 0
    return pl.pallas_call(
        functools.partial(_ffn_kernel, width=width, tm=tm, n_chunks=nf, alpha=alpha, halo=halo),
        grid=(bsz, nt),
        in_specs=in_specs,
        out_specs=pl.BlockSpec((1, tm, d), lambda b, i: (b, i, 0)),
        out_shape=jax.ShapeDtypeStruct((bsz, length, d), F32),
        scratch_shapes=[pltpu.VMEM((tm + 2 * hw, d), BF16),
                        pltpu.VMEM((2, tm + 2 * hw + 16, fc), F32),
                        pltpu.VMEM((2, tm, fc), F32),
                        pltpu.VMEM((2, tm, fc), BF16),
                        pltpu.VMEM((tm, d), F32)],
        compiler_params=_cparams("parallel", "parallel"),
        name="conv_ffn_ln",
    )(*args)


def _s5_operators(lam_re, lam_im, log_dt, b_re, b_im, c_re, c_im):
    hi = lax.Precision.HIGHEST
    s = S5_SUB
    e_cols, f_rows, m_tot, a_rows = [], [], None, []
    for direction in range(2):
        lre = jnp.minimum(lam_re[direction], -1e-4)
        lim = lam_im[direction]
        dt = jnp.exp(log_dt[direction])[:, None]
        mag = jnp.exp(lre * dt)
        ar, ai = mag * jnp.cos(lim * dt), mag * jnp.sin(lim * dt)
        nr, ni = ar - 1.0, ai
        den = lre * lre + lim * lim
        cr = (nr * lre + ni * lim) / den
        ci = (ni * lre - nr * lim) / den
        bbr = cr[..., None] * b_re[direction] - ci[..., None] * b_im[direction]
        bbi = cr[..., None] * b_im[direction] + ci[..., None] * b_re[direction]
        pr, pi = [jnp.ones_like(ar)], [jnp.zeros_like(ai)]
        for _ in range(s):
            pr.append(pr[-1] * ar - pi[-1] * ai)
            pi.append(pr[-2] * ai + pi[-1] * ar)
        pr, pi = jnp.stack(pr), jnp.stack(pi)
        abr = pr[:s, :, :, None] * bbr - pi[:s, :, :, None] * bbi
        abi = pr[:s, :, :, None] * bbi + pi[:s, :, :, None] * bbr
        ccr, cci = c_re[direction], c_im[direction]
        kern = (jnp.einsum('gop,tgpi->tgoi', ccr, abr, precision=hi)
                - jnp.einsum('gop,tgpi->tgoi', cci, abi, precision=hi))
        j = np.arange(s)[:, None]
        t = np.arange(s)[None, :]
        lag = (t - j) if direction == 0 else (j - t)
        place = jnp.asarray(lag[:, :, None] == np.arange(s), F32)
        m = jnp.einsum('jtz,zgoi->gjito', place, kern, precision=hi)
        m_tot = m if m_tot is None else m_tot + m
        flip = (lambda a: a[::-1]) if direction == 0 else (lambda a: a)
        er = flip(abr).transpose(1, 0, 3, 2)
        ei = flip(abi).transpose(1, 0, 3, 2)
        e_cols.append((er, ei))
        zr, zi = (pr[1:], pi[1:]) if direction == 0 else (pr[:0:-1], pi[:0:-1])
        f_re = jnp.einsum('gop,tgp->gpto', ccr, zr) - jnp.einsum('gop,tgp->gpto', cci, zi)
        f_im = -(jnp.einsum('gop,tgp->gpto', ccr, zi) + jnp.einsum('gop,tgp->gpto', cci, zr))
        f_rows.append((f_re, f_im))
        a_rows.append((jnp.concatenate([pr[s], pr[s]], -1), jnp.concatenate([-pi[s], pi[s]], -1)))
    g = lam_re.shape[1]
    w = s * S5_GROUP
    (fer, fei), (ber, bei) = e_cols
    e_mat = jnp.concatenate([fer, fei, ber, bei, fei, fer, bei, ber], axis=-1).reshape(g, w, -1)
    (ffr, ffi), (bfr, bfi) = f_rows
    f_mat = jnp.concatenate([ffr, ffi, bfr, bfi], axis=1).reshape(g, -1, w)
    a_mat = jnp.stack([a_rows[0][0], a_rows[0][1], a_rows[1][0], a_rows[1][1]], axis=1)
    return e_mat, m_tot.reshape(g, w, w), f_mat, a_mat


def _pick_group(res, n, gp, bsz):
    if gp == 1:
        return res
    sel = (lax.broadcasted_iota(jnp.int32, (res.shape[0], n), 0) & 7) // bsz
    out = res[:, :n]
    for k in range(1, gp):
        out = jnp.where(sel == k, res[:, k * n:(k + 1) * n], out)
    return out


def _s5_kernel(xc_ref, xl_ref, d_ref, a_ref, e_ref, m_ref, f_ref, yc_ref, yl_ref, se, *, bsz, gp, rc):
    hv = xc_ref.shape[1]
    rows_c, w = xc_ref.shape[2], hv * xc_ref.shape[3]
    rows_l = xl_ref.shape[2]
    ln = w // hv

    def rows_of(x_ref, r0, n):
        return jnp.concatenate([x_ref[0, hf, pl.ds(r0, n), :] for hf in range(hv)], axis=1)
    p2 = a_ref.shape[-1]

    def chunks(rows):
        return [(r0, min(rc, rows - r0)) for r0 in range(0, rows, rc)]

    segments = ((xc_ref, yc_ref, 0, rows_c), (xl_ref, yl_ref, rows_c, rows_l))

    for x_ref, _, base, rows in segments:
        for r0, n in chunks(rows):
            u = rows_of(x_ref, r0, n).astype(BF16)
            e = jnp.dot(u, e_ref[0], preferred_element_type=F32)
            se[base + r0:base + r0 + n, :] = _pick_group(e, 4 * p2, gp, bsz)

    a1f, a2f, a1b, a2b = a_ref[0, 0], a_ref[0, 1], a_ref[0, 2], a_ref[0, 3]

    def make_step(base, tiles):
        def step(s, carry):
            vf, wf, vb, wb = carry
            rf = pl.multiple_of(base + s * 8, 8)
            rb = pl.multiple_of(base + (tiles - 1 - s) * 8, 8)
            ef_v = se[pl.ds(rf, 8), 0:p2]
            ef_w = se[pl.ds(rf, 8), 2 * p2:3 * p2]
            eb_v = se[pl.ds(rb, 8), p2:2 * p2]
            eb_w = se[pl.ds(rb, 8), 3 * p2:4 * p2]
            se[pl.ds(rf, 8), 0:p2] = vf
            se[pl.ds(rb, 8), p2:2 * p2] = vb
            return (a1f * vf + a2f * wf + ef_v, a1f * wf - a2f * vf + ef_w,
                    a1b * vb + a2b * wb + eb_v, a1b * wb - a2b * vb + eb_w)
        return step

    zero = jnp.zeros((8, p2), F32)
    carry = (zero, zero, zero, zero)
    carry = lax.fori_loop(0, rows_c // 8, make_step(0, rows_c // 8), carry, unroll=2)
    lax.fori_loop(0, rows_l // 8, make_step(rows_c, rows_l // 8), carry, unroll=2)

    for x_ref, y_ref, base, rows in segments:
        for r0, n in chunks(rows):
            u = rows_of(x_ref, r0, n)
            s_in = se[base + r0:base + r0 + n, 0:2 * p2].astype(BF16)
            y = (_pick_group(jnp.dot(u.astype(BF16), m_ref[0], preferred_element_type=F32), w, gp, bsz)
                 + _pick_group(jnp.dot(s_in, f_ref[0], preferred_element_type=F32), w, gp, bsz))
            y = y + (u.reshape(n // 8, 8, w) * d_ref[0]).reshape(n, w)
            y = _gelu(y)
            for hf in range(hv):
                y_ref[0, hf, pl.ds(r0, n), :] = y[:, hf * ln:(hf + 1) * ln]


def _chunk_butterfly(lo, hi):
    n = len(lo)
    lanes = lo[0].shape[1]
    chunk = lax.broadcasted_iota(jnp.int32, lo[0].shape, 1) // S5_GROUP
    m = 1
    while m < n // 2:
        odd = (chunk & m) != 0

        def exchange(v, m=m, odd=odd):
            out = []
            for t in range(n):
                if t & m == 0:
                    out.append(jnp.where(odd, pltpu.roll(v[t + m], S5_GROUP * m, 1), v[t]))
                else:
                    out.append(jnp.where(odd, v[t], pltpu.roll(v[t - m], lanes - S5_GROUP * m, 1)))
            return out

        lo, hi = exchange(lo), exchange(hi)
        m *= 2
    h = n // 2
    return ([lo[t] if t < h else hi[t - h] for t in range(n)],
            [lo[t + h] if t < h else hi[t] for t in range(n)])


def _s5_modulated_rows(x_refs, mod_ref, b, load):
    ln = x_refs[0].shape[-1]
    mb = min(b, mod_ref.shape[0] - 1)
    halves = []
    for hf, x_ref in enumerate(x_refs):
        sc = 1.0 + mod_ref[mb, 1:2, hf * ln:(hf + 1) * ln]
        sh = mod_ref[mb, 0:1, hf * ln:(hf + 1) * ln]
        halves.append([load(x_ref, t) * sc + sh for t in range(S5_SUB)])
    return halves


def _s5_pack_kernel(xa_ref, xb_ref, mod_ref, o_ref, *, bsz, gp):
    n = S5_SUB

    def body(i, carry):
        for b in range(bsz):
            lo, hi = _chunk_butterfly(*_s5_modulated_rows(
                (xa_ref, xb_ref), mod_ref, b, lambda x_ref, t: x_ref[b, pl.ds(i * 8 * n + t, 8, stride=n), :]))
            for g in range(n):
                tile, gs = divmod(g, gp)
                rows = pl.ds(i * 64 + gs * bsz + b, 8, stride=8)
                o_ref[tile, 0, rows, :] = lo[g]
                o_ref[tile, 1, rows, :] = hi[g]
        return carry

    lax.fori_loop(0, xa_ref.shape[1] // (8 * n), body, 0)


def _s5_pack_cm_kernel(xa_ref, xb_ref, mod_ref, o_ref, *, bsz, gp):
    n = S5_SUB
    nrk = xa_ref.shape[1] // n

    def body(rk, carry):
        for b in range(bsz):
            lo, hi = _chunk_butterfly(*_s5_modulated_rows(
                (xa_ref, xb_ref), mod_ref, b, lambda x_ref, t: x_ref[b, rk * n + t]))
            for g in range(n):
                tile, gs = divmod(g, gp)
                rows = pl.ds(rk * 8 + gs * bsz + b, 8, stride=nrk * 8)
                o_ref[tile, 0, rows, :] = lo[g]
                o_ref[tile, 1, rows, :] = hi[g]
        return carry

    lax.fori_loop(0, nrk, body, 0)


def _s5_unpack_kernel(y_ref, o_ref, obuf, *, bsz, gp):
    n = S5_SUB

    def body(i, carry):
        for b in range(bsz):
            rows = lambda g: pl.ds(i * 64 + (g % gp) * bsz + b, 8, stride=8)
            lo, hi = _chunk_butterfly([y_ref[g // gp, 0, rows(g), :] for g in range(n)],
                                      [y_ref[g // gp, 1, rows(g), :] for g in range(n)])
            for t in range(n):
                tok = pl.ds(i * 8 * n + t, 8, stride=n)
                obuf[b, 0, tok, :] = lo[t]
                obuf[b, 1, tok, :] = hi[t]
        return carry

    lax.fori_loop(0, o_ref.shape[1] // (8 * n), body, 0)
    for b in range(bsz):
        o_ref[b] = jnp.concatenate([obuf[b, 0], obuf[b, 1]], axis=1).astype(o_ref.dtype)


def _s5_unpack_cm_kernel(y_ref, o_ref, *, bsz, gp):
    n = S5_SUB
    nrk = o_ref.shape[1] // n
    ln = y_ref.shape[-1]

    def body(rk, carry):
        for b in range(bsz):
            rows = lambda g: pl.ds(rk * 8 + (g % gp) * bsz + b, 8, stride=nrk * 8)
            lo, hi = _chunk_butterfly([y_ref[g // gp, 0, rows(g), :] for g in range(n)],
                                      [y_ref[g // gp, 1, rows(g), :] for g in range(n)])
            for t in range(n):
                o_ref[b, rk * n + t, :, 0:ln] = lo[t]
                o_ref[b, rk * n + t, :, ln:2 * ln] = hi[t]
        return carry

    lax.fori_loop(0, nrk, body, 0)


def _s5_pack(h, mod, gp, grid_rows=None):
    bsz, length, d = h.shape
    w = S5_SUB * S5_GROUP
    ln = S5_LANES
    assert S5_SUB == 16 and S5_GROUP == 16 and w == 2 * ln and d % w == 0
    tiles = S5_SUB // gp
    out_shape = jax.ShapeDtypeStruct((d // S5_GROUP // gp, 2, length // S5_SUB * 8, ln), F32)
    mod_spec = pl.BlockSpec((mod.shape[0], 8, w), lambda q, i: (0, 0, q))
    if grid_rows is None:
        tb = min(length, 512)
        return pl.pallas_call(
            functools.partial(_s5_pack_kernel, bsz=bsz, gp=gp),
            grid=(d // w, length // tb),
            in_specs=[pl.BlockSpec((bsz, tb, ln), lambda q, i: (0, i, 2 * q)),
                      pl.BlockSpec((bsz, tb, ln), lambda q, i: (0, i, 2 * q + 1)), mod_spec],
            out_specs=pl.BlockSpec((tiles, 2, tb // S5_SUB * 8, ln), lambda q, i: (q, 0, i, 0)),
            out_shape=out_shape,
            compiler_params=_cparams("parallel", "parallel"),
            name="s5_pack",
        )(h, h, mod)
    cols = length // grid_rows
    assert grid_rows % S5_SUB == 0 and cols % 8 == 0
    h4 = h.reshape(bsz, grid_rows, cols, d)
    return pl.pallas_call(
        functools.partial(_s5_pack_cm_kernel, bsz=bsz, gp=gp),
        grid=(d // w, cols // 8),
        in_specs=[pl.BlockSpec((bsz, grid_rows, 8, ln), lambda q, i: (0, 0, i, 2 * q)),
                  pl.BlockSpec((bsz, grid_rows, 8, ln), lambda q, i: (0, 0, i, 2 * q + 1)), mod_spec],
        out_specs=pl.BlockSpec((tiles, 2, grid_rows // 2 * 8, ln), lambda q, i: (q, 0, i, 0)),
        out_shape=out_shape,
        compiler_params=_cparams("parallel", "parallel"),
        name="s5_pack_cm",
    )(h4, h4, mod)


def _s5_unpack(y, bsz, gp, grid_rows=None):
    gt, hv, rows, ln = y.shape
    w = hv * ln
    length = rows // 8 * S5_SUB
    d = gt * gp * S5_GROUP
    tiles = S5_SUB // gp
    if grid_rows is None:
        tb = min(length, 512)
        return pl.pallas_call(
            functools.partial(_s5_unpack_kernel, bsz=bsz, gp=gp),
            grid=(d // w, length // tb),
            in_specs=[pl.BlockSpec((tiles, hv, tb // S5_SUB * 8, ln), lambda q, i: (q, 0, i, 0))],
            out_specs=pl.BlockSpec((bsz, tb, w), lambda q, i: (0, i, q)),
            out_shape=jax.ShapeDtypeStruct((bsz, length, d), BF16),
            scratch_shapes=[pltpu.VMEM((bsz, hv, tb, ln), F32)],
            compiler_params=_cparams("parallel", "parallel"),
            name="s5_unpack",
        )(y)
    cols = length // grid_rows
    out = pl.pallas_call(
        functools.partial(_s5_unpack_cm_kernel, bsz=bsz, gp=gp),
        grid=(d // w, cols // 8),
        in_specs=[pl.BlockSpec((tiles, hv, grid_rows // 2 * 8, ln), lambda q, i: (q, 0, i, 0))],
        out_specs=pl.BlockSpec((bsz, grid_rows, 8, w), lambda q, i: (0, 0, i, q)),
        out_shape=jax.ShapeDtypeStruct((bsz, grid_rows, cols, d), F32),
        compiler_params=_cparams("parallel", "parallel"),
        name="s5_unpack_cm",
    )(y)
    return out.reshape(bsz, length, d)


def _s5_pattern(v, gp, reps):
    d = v.shape[-1]
    gt = d // S5_GROUP // gp
    x = jnp.broadcast_to(v.reshape(gt, gp, 1, 1, S5_GROUP), (gt, gp, reps, S5_SUB, S5_GROUP))
    return x.reshape(gt, gp * reps, S5_SUB * S5_GROUP)


def _grid_transpose(x, rows, cols):
    bsz, _, ch = x.shape
    return x.reshape(bsz, rows, cols, ch).transpose(0, 2, 1, 3).reshape(bsz, rows * cols, ch)


def _s5_core(h_lat, h_ctx, mod_l, mod_c, params, col_major):
    lam_re, lam_im, log_dt, b_re, b_im, c_re, c_im, d_skip = params
    bsz, length, d = h_lat.shape
    assert 8 % bsz == 0 and S5_SUB % (8 // bsz) == 0
    gp = 8 // bsz
    gt = d // S5_GROUP // gp
    w = S5_SUB * S5_GROUP
    grid_rows = (length // GRID_W) if col_major else None
    xl = _s5_pack(h_lat, mod_l, gp, grid_rows)
    xc = _s5_pack(h_ctx, mod_c, gp)
    e_mat, m_mat, f_mat, a_mat = _s5_operators(lam_re, lam_im, log_dt, b_re, b_im, c_re, c_im)
    p2 = a_mat.shape[-1]
    cat = lambda m: m.reshape(gt, gp, m.shape[1], m.shape[2]).transpose(0, 2, 1, 3).reshape(gt, m.shape[1], -1)
    e_cat, m_cat, f_cat = (cat(m).astype(BF16) for m in (e_mat, m_mat, f_mat))
    a_pat = jnp.broadcast_to(a_mat.reshape(gt, gp, 1, 4, p2), (gt, gp, bsz, 4, p2)).transpose(0, 3, 1, 2, 4)
    a_pat = a_pat.reshape(gt, 4, 8, p2)
    d_pat = _s5_pattern(d_skip, gp, bsz)
    hv, rows_l, rows_c, ln = xl.shape[1], xl.shape[2], xc.shape[2], xl.shape[3]
    rc = 512
    tile = lambda *shape: pl.BlockSpec((1,) + shape, lambda g: (g,) + (0,) * len(shape))
    yc, yl = pl.pallas_call(
        functools.partial(_s5_kernel, bsz=bsz, gp=gp, rc=rc),
        grid=(gt,),
        in_specs=[tile(hv, rows_c, ln), tile(hv, rows_l, ln), tile(8, w), tile(4, 8, p2),
                  tile(w, gp * 4 * p2), tile(w, gp * w), tile(2 * p2, gp * w)],
        out_specs=[tile(hv, rows_c, ln), tile(hv, rows_l, ln)],
        out_shape=[jax.ShapeDtypeStruct((gt, hv, rows_c, ln), F32),
                   jax.ShapeDtypeStruct((gt, hv, rows_l, ln), F32)],
        scratch_shapes=[pltpu.VMEM((rows_c + rows_l, 4 * p2), F32)],
        compiler_params=_cparams("parallel"),
        name="s5_scan",
    )(xc, xl, d_pat, a_pat, e_cat, m_cat, f_cat)
    return _s5_unpack(yl, bsz, gp, grid_rows), _s5_unpack(yc, bsz, gp)


def _glu_ln_kernel(y_ref, h_ref, mod_ref, w_ref, b_ref, lng_ref, lnb_ref, o_ref, *, alpha):
    d = h_ref.shape[-1]
    z = jnp.dot(y_ref[0].astype(BF16), w_ref[...], preferred_element_type=F32) + b_ref[...]
    glu = z[:, :d] * _sigmoid(z[:, d:])
    g1 = 1.0 + mod_ref[0, 2:3, :]
    o_ref[0] = _ln(alpha * h_ref[0] + g1 * glu, lng_ref[...], lnb_ref[...])


def _glu_ln(y, h, mod, w_glu, b_glu, ln_g, ln_b, *, alpha, mod_row=None):
    bsz, length, d = h.shape
    tm = min(length, 512)
    mod_map = (lambda b, i: (b, 0, 0)) if mod_row is None else (lambda b, i: (mod_row, 0, 0))
    tok = lambda: pl.BlockSpec((1, tm, d), lambda b, i: (b, i, 0))
    return pl.pallas_call(
        functools.partial(_glu_ln_kernel, alpha=alpha),
        grid=(bsz, length // tm),
        in_specs=[tok(), tok(), pl.BlockSpec((1, 8, d), mod_map), _const_spec((d, 2 * d)), _const_spec((1, 2 * d)),
                  _const_spec((1, d)), _const_spec((1, d))],
        out_specs=tok(),
        out_shape=jax.ShapeDtypeStruct((bsz, length, d), F32),
        compiler_params=_cparams("parallel", "parallel"),
        name="glu_ln",
    )(y, h, mod, w_glu.astype(BF16), b_glu.reshape(1, 2 * d), ln_g.reshape(1, d), ln_b.reshape(1, d))


def _mod_map(mod_row):
    return (lambda b, i: (b, 0, 0)) if mod_row is None else (lambda b, i: (mod_row, 0, 0))


def _ml_up_kernel(h_ref, mod_ref, w_ref, xm_ref, z_ref):
    e = xm_ref.shape[-1]
    u = (h_ref[0] * (1.0 + mod_ref[0, 1:2, :]) + mod_ref[0, 0:1, :]).astype(BF16)
    y = jnp.dot(u, w_ref[...], preferred_element_type=F32)
    xm_ref[0] = y[:, :e]
    z_ref[0] = y[:, e:].astype(BF16)


def _ml_up(h, mod, w_up, *, mod_row=None):
    bsz, length, d = h.shape
    e = w_up.shape[1] // 2
    tm = min(length, 512)
    tok = lambda n: pl.BlockSpec((1, tm, n), lambda b, i: (b, i, 0))
    return pl.pallas_call(
        _ml_up_kernel,
        grid=(bsz, length // tm),
        in_specs=[tok(d), pl.BlockSpec((1, 8, d), _mod_map(mod_row)), _const_spec((d, 2 * e))],
        out_specs=[tok(e), tok(e)],
        out_shape=[jax.ShapeDtypeStruct((bsz, length, e), F32), jax.ShapeDtypeStruct((bsz, length, e), BF16)],
        compiler_params=_cparams("parallel", "parallel"),
        name="ml_up",
    )(h, mod, w_up.astype(BF16))


def _ml_qkv_kernel(prev_ref, x_ref, next_ref, cw_ref, cb_ref, w_ref, wg_ref, bg_ref,
                   xc_ref, q_ref, k_ref, v_ref, g_ref, xbuf, *, k_scale):
    i = pl.program_id(1)
    nt = pl.num_programs(1)
    tm, e = x_ref.shape[1], x_ref.shape[2]
    hp = prev_ref.shape[1]
    xbuf[0:hp, :] = jnp.where(i > 0, prev_ref[0], 0.0)
    xbuf[hp:hp + tm, :] = x_ref[0]
    xbuf[hp + tm:, :] = jnp.where(i < nt - 1, next_ref[0], 0.0)
    x = x_ref[0]
    conv = (cw_ref[0:1, :] * xbuf[hp - 1:hp - 1 + tm, :] + cw_ref[1:2, :] * x
            + cw_ref[2:3, :] * xbuf[hp + 1:hp + 1 + tm, :] + cb_ref[...])
    xc = _silu(conv).astype(BF16)
    xc_ref[0] = xc
    q = jnp.dot(xc, w_ref[0], preferred_element_type=F32).astype(BF16)
    k = (jnp.dot(xc, w_ref[1], preferred_element_type=F32) * k_scale).astype(BF16)
    v = jnp.dot(x.astype(BF16), w_ref[2], preferred_element_type=F32).astype(BF16)
    q_ref[0] = q
    k_ref[0] = k
    v_ref[0] = v
    g_ref[0] = (jnp.dot(q, wg_ref[0], preferred_element_type=F32) + jnp.dot(k, wg_ref[1], preferred_element_type=F32)
                + jnp.dot(v, wg_ref[2], preferred_element_type=F32) + bg_ref[...])


def _ml_qkv(xm, conv_w, conv_b, w_q, w_k, w_v, w_gates, b_gates):
    bsz, length, e = xm.shape
    tm = min(length, 256)
    hp = 8
    r = tm // hp
    nb = length // hp
    ng = w_gates.shape[-1]
    lanes = 128
    w3 = jnp.stack([w_q, w_k, w_v]).astype(BF16)
    wg = jnp.zeros((3, e, lanes), BF16).at[:, :, :ng].set(w_gates.astype(BF16))
    bg = jnp.zeros((1, lanes), F32).at[0, :ng].set(b_gates)
    cw = jnp.zeros((8, e), F32).at[:conv_w.shape[0]].set(conv_w)
    tok = lambda n: pl.BlockSpec((1, tm, n), lambda b, i: (b, i, 0))
    return pl.pallas_call(
        functools.partial(_ml_qkv_kernel, k_scale=(e // ML_HEADS) ** -0.5),
        grid=(bsz, length // tm),
        in_specs=[pl.BlockSpec((1, hp, e), lambda b, i: (b, jnp.maximum(i * r - 1, 0), 0)), tok(e),
                  pl.BlockSpec((1, hp, e), lambda b, i: (b, jnp.minimum((i + 1) * r, nb - 1), 0)),
                  _const_spec((8, e)), _const_spec((1, e)), _const_spec((3, e, e)), _const_spec((3, e, lanes)),
                  _const_spec((1, lanes))],
        out_specs=[tok(e), tok(e), tok(e), tok(e), tok(lanes)],
        out_shape=[jax.ShapeDtypeStruct((bsz, length, e), BF16)] * 4
                  + [jax.ShapeDtypeStruct((bsz, length, lanes), F32)],
        scratch_shapes=[pltpu.VMEM((tm + 2 * hp, e), F32)],
        compiler_params=_cparams("parallel", "parallel"),
        name="ml_qkv",
    )(xm, xm, xm, cw, conv_b.reshape(1, e), w3, wg, bg)


def _log_sigmoid(x):
    return jnp.minimum(x, 0.0) - jnp.log(1.0 + jnp.exp(-jnp.abs(x)))


def _ml_scan_kernel(*refs, nblk_c, heads):
    (qcf, kcf, vcf, gccf, grcf, qcb, kcb, vcb, gccb, grcb,
     qlf, klf, vlf, gclf, grlf, qlb, klb, vlb, gclb, grlb,
     hf_ref, hb_ref, c_sc, cb_sc, n_sc, m_sc) = refs
    i = pl.program_id(1)
    is_ctx = i < nblk_c
    t = qcf.shape[1]
    e = qcf.shape[2]
    dh = e // heads

    @pl.when(i == 0)
    def _():
        c_sc[...] = jnp.zeros_like(c_sc)
        cb_sc[...] = jnp.zeros_like(cb_sc)
        n_sc[...] = jnp.zeros_like(n_sc)
        m_sc[...] = jnp.zeros_like(m_sc)

    row = lax.broadcasted_iota(jnp.int32, (t, t), 0)
    col = lax.broadcasted_iota(jnp.int32, (t, t), 1)
    pick = lambda c_ref, l_ref: jnp.where(is_ctx, c_ref[0], l_ref[0])
    for direction, blk in enumerate(((qcf, kcf, vcf, gccf, grcf, qlf, klf, vlf, gclf, grlf),
                                     (qcb, kcb, vcb, gccb, grcb, qlb, klb, vlb, gclb, grlb))):
        q_all, k_all, v_all, gc, gr = (pick(blk[j], blk[j + 5]) for j in range(5))
        mask = (col <= row) if direction == 0 else (col >= row)
        mask_t = (row <= col) if direction == 0 else (row >= col)
        h_ref = hf_ref if direction == 0 else hb_ref
        for head in range(heads):
            r = direction * heads + head
            ci = direction * 2 * heads + head
            cf = ci + heads
            q = q_all[:, head * dh:(head + 1) * dh]
            k = k_all[:, head * dh:(head + 1) * dh]
            v = v_all[:, head * dh:(head + 1) * dh]
            ig_col, ig_row = gc[:, ci:ci + 1], gr[ci:ci + 1, :]
            lf_col, lf_row = _log_sigmoid(gc[:, cf:cf + 1]), _log_sigmoid(gr[cf:cf + 1, :])
            m_prev = m_sc[r, :, 0:1]
            b_col = jnp.sum(jnp.where(mask, jnp.broadcast_to(lf_row, (t, t)), 0.0), axis=1, keepdims=True)
            b_row = jnp.sum(jnp.where(mask_t, jnp.broadcast_to(lf_col, (t, t)), 0.0), axis=0, keepdims=True)
            b_tot = jnp.sum(lf_row, axis=1, keepdims=True)
            logd = jnp.where(mask, b_col - b_row + ig_row, -jnp.inf)
            m_inter = b_col + m_prev
            m_t = jnp.maximum(m_inter, jnp.max(logd, axis=1, keepdims=True))
            w_inter = jnp.exp(m_inter - m_t)
            qk = lax.dot_general(q, k, (((1,), (1,)), ((), ())), preferred_element_type=F32)
            s = jnp.exp(logd - m_t) * qk
            n_row = n_sc[r]
            num = (w_inter * jnp.dot(q, cb_sc[r], preferred_element_type=F32)
                   + jnp.dot(s.astype(BF16), v, preferred_element_type=F32))
            den = (w_inter * jnp.sum(q.astype(F32) * n_row, axis=1, keepdims=True)
                   + jnp.sum(s, axis=1, keepdims=True))
            h = num / jnp.maximum(jnp.abs(den), jnp.exp(-m_t))
            h_ref[0, :, head * dh:(head + 1) * dh] = h.astype(h_ref.dtype)
            decay = b_tot - b_col + ig_col
            m_new = jnp.maximum(b_tot + m_prev, jnp.max(decay, axis=0, keepdims=True))
            w_prev = jnp.exp(b_tot + m_prev - m_new)
            w_r = jnp.exp(decay - m_new)
            wv = (w_r * v.astype(F32)).astype(BF16)
            c_new = w_prev * c_sc[r] + lax.dot_general(k, wv, (((0,), (0,)), ((), ())), preferred_element_type=F32)
            c_sc[r] = c_new
            cb_sc[r] = c_new.astype(BF16)
            n_sc[r] = w_prev * n_row + jnp.sum(w_r * k.astype(F32), axis=0, keepdims=True)
            m_sc[r] = jnp.broadcast_to(m_new, m_sc.shape[1:])


def _ml_scan(qkv_c, g_c, qkv_l, g_l):
    bsz, len_c, e = qkv_c[0].shape
    len_l = qkv_l[0].shape[1]
    t = SCAN_BLOCK
    nc, nl = len_c // t, len_l // t
    lanes = g_c.shape[-1]
    rows = 16
    grow = lambda g: jnp.swapaxes(g[:, :, :rows], 1, 2)
    cf = lambda i: jnp.minimum(i, nc - 1)
    cb = lambda i: jnp.maximum(nc - 1 - i, 0)
    lf = lambda i: jnp.maximum(i - nc, 0)
    lb = lambda i: jnp.minimum(nl - 1 - (i - nc), nl - 1)
    def specs(idx):
        tok = pl.BlockSpec((1, t, e), lambda b, i: (b, idx(i), 0))
        return [tok, tok, tok, pl.BlockSpec((1, t, lanes), lambda b, i: (b, idx(i), 0)),
                pl.BlockSpec((1, rows, t), lambda b, i: (b, 0, idx(i)))]
    args_c = list(qkv_c) + [g_c, grow(g_c)]
    args_l = list(qkv_l) + [g_l, grow(g_l)]
    fwd = lambda i: i
    bwd = lambda i: jnp.where(i < nc, nc - 1 - i, 2 * nc + nl - 1 - i)
    out = lambda idx: pl.BlockSpec((1, t, e), lambda b, i: (b, idx(i), 0))
    heads = ML_HEADS
    dh = e // heads
    return pl.pallas_call(
        functools.partial(_ml_scan_kernel, nblk_c=nc, heads=heads),
        grid=(bsz, nc + nl),
        in_specs=specs(cf) + specs(cb) + specs(lf) + specs(lb),
        out_specs=[out(fwd), out(bwd)],
        out_shape=[jax.ShapeDtypeStruct((bsz, len_c + len_l, e), BF16)] * 2,
        scratch_shapes=[pltpu.VMEM((2 * heads, dh, dh), F32), pltpu.VMEM((2 * heads, dh, dh), BF16),
                        pltpu.VMEM((2 * heads, 1, dh), F32), pltpu.VMEM((2 * heads, 1, 128), F32)],
        compiler_params=_cparams("parallel", "arbitrary"),
        name="ml_scan",
    )(*(args_c + args_c + args_l + args_l))


def _ml_out_kernel(hf_ref, hb_ref, xc_ref, z_ref, h_ref, mod_ref, gn_ref, sk_ref, w_ref, lng_ref, lnb_ref, o_ref,
                   *, alpha, heads):
    e = hf_ref.shape[-1]
    dh = e // heads
    hh = hf_ref[0].astype(F32) + hb_ref[0].astype(F32)
    parts = []
    for head in range(heads):
        x = hh[:, head * dh:(head + 1) * dh]
        mu = jnp.mean(x, axis=-1, keepdims=True)
        xc = x - mu
        var = jnp.mean(xc * xc, axis=-1, keepdims=True)
        parts.append(xc * lax.rsqrt(var + LN_EPS))
    hn = jnp.concatenate(parts, axis=1) * gn_ref[...] + sk_ref[...] * xc_ref[0].astype(F32)
    a = (hn * _silu(z_ref[0].astype(F32))).astype(BF16)
    y = jnp.dot(a, w_ref[...], preferred_element_type=F32)
    g1 = 1.0 + mod_ref[0, 2:3, :]
    o_ref[0] = _ln(alpha * h_ref[0] + g1 * y, lng_ref[...], lnb_ref[...])


def _ml_out(hf, hb, xc, z, h, mod, gn_g, skip, w_down, ln_g, ln_b, *, alpha, mod_row=None, offset=0):
    bsz, length, d = h.shape
    e = hf.shape[-1]
    tm = min(length, 256)
    assert offset % tm == 0
    tok = lambda n: pl.BlockSpec((1, tm, n), lambda b, i: (b, i, 0))
    hid = pl.BlockSpec((1, tm, e), lambda b, i: (b, i + offset // tm, 0))
    return pl.pallas_call(
        functools.partial(_ml_out_kernel, alpha=alpha, heads=ML_HEADS),
        grid=(bsz, length // tm),
        in_specs=[hid, hid, tok(e), tok(e), tok(d), pl.BlockSpec((1, 8, d), _mod_map(mod_row)),
                  _const_spec((1, e)), _const_spec((1, e)), _const_spec((e, d)), _const_spec((1, d)),
                  _const_spec((1, d))],
        out_specs=tok(d),
        out_shape=jax.ShapeDtypeStruct((bsz, length, d), F32),
        compiler_params=_cparams("parallel", "parallel"),
        name="ml_out",
    )(hf, hb, xc, z, h, mod, gn_g.reshape(1, e), skip.reshape(1, e), w_down.astype(BF16),
      ln_g.reshape(1, d), ln_b.reshape(1, d))


def _mlstm_layer(h_lat, h_ctx, mod_l, mod_c, p, ln_g, ln_b, *, alpha, need_ctx):
    (w_up, conv_w, conv_b, w_q, w_k, w_v, w_gates, b_gates, gn_g, skip, w_down) = p
    xm_l, z_l = _ml_up(h_lat, mod_l, w_up)
    xm_c, z_c = _ml_up(h_ctx, mod_c, w_up, mod_row=0)
    xc_l, q_l, k_l, v_l, g_l = _ml_qkv(xm_l, conv_w, conv_b, w_q, w_k, w_v, w_gates, b_gates)
    xc_c, q_c, k_c, v_c, g_c = _ml_qkv(xm_c, conv_w, conv_b, w_q, w_k, w_v, w_gates, b_gates)
    hf, hb = _ml_scan((q_c, k_c, v_c), g_c, (q_l, k_l, v_l), g_l)
    out_l = _ml_out(hf, hb, xc_l, z_l, h_lat, mod_l, gn_g, skip, w_down, ln_g, ln_b, alpha=alpha,
                    offset=h_ctx.shape[1])
    out_c = None
    if need_ctx:
        out_c = _ml_out(hf, hb, xc_c, z_c, h_ctx, mod_c, gn_g, skip, w_down, ln_g, ln_b, alpha=alpha, mod_row=0)
    return out_l, out_c


def _cv_in_kernel(h_ref, mod_ref, w_ref, b_ref, o_ref):
    d = h_ref.shape[-1]
    u = (h_ref[0] * (1.0 + mod_ref[0, 1:2, :]) + mod_ref[0, 0:1, :]).astype(BF16)
    y = jnp.dot(u, w_ref[...], preferred_element_type=F32) + b_ref[...]
    o_ref[0] = y[:, :d] * _sigmoid(y[:, d:])


def _cv_in(h, mod, w_in, b_in, *, mod_row=None):
    bsz, length, d = h.shape
    tm = min(length, 512)
    tok = lambda: pl.BlockSpec((1, tm, d), lambda b, i: (b, i, 0))
    return pl.pallas_call(
        _cv_in_kernel,
        grid=(bsz, length // tm),
        in_specs=[tok(), pl.BlockSpec((1, 8, d), _mod_map(mod_row)), _const_spec((d, 2 * d)),
                  _const_spec((1, 2 * d))],
        out_specs=tok(),
        out_shape=jax.ShapeDtypeStruct((bsz, length, d), F32),
        compiler_params=_cparams("parallel", "parallel"),
        name="cv_in",
    )(h, mod, w_in.astype(BF16), b_in.reshape(1, 2 * d))


def _cv_out_kernel(prev_ref, x_ref, next_ref, h_ref, mod_ref, dw_ref, db_ref, cg_ref, cb_ref, w_ref, b_ref,
                   lng_ref, lnb_ref, o_ref, sbuf, cbuf, *, alpha, taps):
    i = pl.program_id(1)
    nt = pl.num_programs(1)
    tm = x_ref.shape[1]
    hp = prev_ref.shape[1]
    rows = tm + 2 * hp
    sbuf[0, 0:hp, :] = jnp.where(i > 0, prev_ref[0], 0.0)
    sbuf[0, hp:hp + tm, :] = x_ref[0]
    sbuf[0, hp + tm:, :] = jnp.where(i < nt - 1, next_ref[0], 0.0)
    for s in range(1, 8):
        sbuf[s, 0:rows - 8, :] = sbuf[0, s:s + rows - 8, :]
    left = (taps - 1) // 2
    rb = 16

    def conv_rows(blk, carry):
        r0 = pl.multiple_of(blk * rb, rb)
        acc = None
        for k in range(taps):
            off = hp + k - left
            win = sbuf[off % 8, pl.ds(r0 + (off // 8) * 8, rb), :].reshape(rb // 8, 8, -1)
            term = dw_ref[k] * win
            acc = term if acc is None else acc + term
        cbuf[pl.ds(r0, rb), :] = acc.reshape(rb, -1)
        return carry

    lax.fori_loop(0, tm // rb, conv_rows, 0)
    c = _silu(_ln(cbuf[...] + db_ref[...], cg_ref[...], cb_ref[...])).astype(BF16)
    y = jnp.dot(c, w_ref[...], preferred_element_type=F32) + b_ref[...]
    g1 = 1.0 + mod_ref[0, 2:3, :]
    o_ref[0] = _ln(alpha * h_ref[0] + g1 * y, lng_ref[...], lnb_ref[...])


def _cv_out(hmid, h, mod, dw_w, dw_b, cln_g, cln_b, w_out, b_out, ln_g, ln_b, *, alpha, mod_row=None):
    bsz, length, d = h.shape
    taps = dw_w.shape[0]
    tm = min(length, 256)
    hp = 16
    assert (taps - 1) // 2 <= hp and taps // 2 <= hp
    r = tm // hp
    nb = length // hp
    dw = jnp.broadcast_to(dw_w[:, None, :], (taps, 8, d))
    tok = lambda: pl.BlockSpec((1, tm, d), lambda b, i: (b, i, 0))
    vec = lambda: _const_spec((1, d))
    return pl.pallas_call(
        functools.partial(_cv_out_kernel, alpha=alpha, taps=taps),
        grid=(bsz, length // tm),
        in_specs=[pl.BlockSpec((1, hp, d), lambda b, i: (b, jnp.maximum(i * r - 1, 0), 0)), tok(),
                  pl.BlockSpec((1, hp, d), lambda b, i: (b, jnp.minimum((i + 1) * r, nb - 1), 0)),
                  tok(), pl.BlockSpec((1, 8, d), _mod_map(mod_row)), _const_spec((taps, 8, d)), vec(), vec(), vec(),
                  _const_spec((d, d)), vec(), vec(), vec()],
        out_specs=tok(),
        out_shape=jax.ShapeDtypeStruct((bsz, length, d), F32),
        scratch_shapes=[pltpu.VMEM((8, tm + 2 * hp, d), F32), pltpu.VMEM((tm, d), F32)],
        compiler_params=_cparams("parallel", "parallel"),
        name="cv_out",
    )(hmid, hmid, hmid, h, mod, dw, dw_b.reshape(1, d), cln_g.reshape(1, d), cln_b.reshape(1, d),
      w_out.astype(BF16), b_out.reshape(1, d), ln_g.reshape(1, d), ln_b.reshape(1, d))


def _jln(x, g, b):
    mu = x.mean(-1, keepdims=True)
    var = jnp.square(x - mu).mean(-1, keepdims=True)
    return (x - mu) * lax.rsqrt(var + LN_EPS) * g + b


def _j_dwconv1d(x, w, b):
    k = w.shape[0]
    y = lax.conv_general_dilated(x, w[:, None, :], (1,), [((k - 1) // 2, k // 2)],
                                 dimension_numbers=('NWC', 'WIO', 'NWC'), feature_group_count=x.shape[-1])
    return y + b


def _j_s5_discretise(lam_re, lam_im, log_dt, b_re, b_im):
    lre = jnp.minimum(lam_re, -1e-4)
    lim = lam_im
    dt = jnp.exp(log_dt)[:, None]
    mag = jnp.exp(lre * dt)
    lb_re, lb_im = mag * jnp.cos(lim * dt), mag * jnp.sin(lim * dt)
    nr, ni = lb_re - 1.0, lb_im
    den = lre * lre + lim * lim
    cr = (nr * lre + ni * lim) / den
    ci = (ni * lre - nr * lim) / den
    bb_re = cr[..., None] * b_re - ci[..., None] * b_im
    bb_im = cr[..., None] * b_im + ci[..., None] * b_re
    return lb_re, lb_im, bb_re, bb_im


def _j_combine(e1, e2):
    a1r, a1i, b1r, b1i = e1
    a2r, a2i, b2r, b2i = e2
    return (a2r * a1r - a2i * a1i, a2r * a1i + a2i * a1r,
            a2r * b1r - a2i * b1i + b2r, a2r * b1i + a2i * b1r + b2i)


def _j_s5_scan(u, lb_re, lb_im, bb_re, bb_im, c_re, c_im, s0):
    bsz, length, d = u.shape
    groups = d // S5_GROUP
    nblk = length // 128
    ub = jnp.moveaxis(u.reshape(bsz, nblk, 128, groups, S5_GROUP), 1, 0)

    def step(carry, u_blk):
        sr, si = carry
        bu_re = jnp.einsum('btgs,gps->btgp', u_blk, bb_re)
        bu_im = jnp.einsum('btgs,gps->btgp', u_blk, bb_im)
        a_re = jnp.broadcast_to(lb_re, bu_re.shape)
        a_im = jnp.broadcast_to(lb_im, bu_im.shape)
        pa_re, pa_im, pb_re, pb_im = lax.associative_scan(_j_combine, (a_re, a_im, bu_re, bu_im), axis=1)
        st_re = pa_re * sr[:, None] - pa_im * si[:, None] + pb_re
        st_im = pa_re * si[:, None] + pa_im * sr[:, None] + pb_im
        y = jnp.einsum('gsp,btgp->btgs', c_re, st_re) - jnp.einsum('gsp,btgp->btgs', c_im, st_im)
        return (st_re[:, -1], st_im[:, -1]), y.reshape(bsz, 128, d)

    s_end, ys = lax.scan(step, s0, ub)
    return jnp.moveaxis(ys, 0, 1).reshape(bsz, length, d), s_end


def _j_s5_mixer(ul, uc, lam_re, lam_im, log_dt, b_re, b_im, c_re, c_im, d_skip, w_glu, b_glu):
    y_lat, y_ctx = d_skip * ul, d_skip * uc
    bsz = ul.shape[0]
    groups, p = lam_re.shape[1], lam_re.shape[2]
    for direction in range(2):
        lb_re, lb_im, bb_re, bb_im = _j_s5_discretise(lam_re[direction], lam_im[direction], log_dt[direction],
                                                      b_re[direction], b_im[direction])
        cr, ci = c_re[direction], c_im[direction]
        zero = jnp.zeros((bsz, groups, p), F32)
        rev = (lambda a: a[:, ::-1]) if direction == 1 else (lambda a: a)
        yc, s_ctx = _j_s5_scan(rev(uc), lb_re, lb_im, bb_re, bb_im, cr, ci, (zero, zero))
        yl, _ = _j_s5_scan(rev(ul), lb_re, lb_im, bb_re, bb_im, cr, ci, s_ctx)
        y_ctx = y_ctx + rev(yc)
        y_lat = y_lat + rev(yl)

    def glu(y):
        z = jax.nn.gelu(y) @ w_glu + b_glu
        val, gate = jnp.split(z, 2, axis=-1)
        return val * jax.nn.sigmoid(gate)

    return glu(y_lat), glu(y_ctx)


def _j_mlstm_scan(q, k, v, ig, lf, state):
    bsz, nh, length, dh = q.shape
    nblk = length // SCAN_BLOCK
    blocks = lambda a: jnp.moveaxis(a.reshape(bsz, nh, nblk, SCAN_BLOCK, *a.shape[3:]), 2, 0)
    tri = jnp.tril(jnp.ones((SCAN_BLOCK, SCAN_BLOCK), bool))

    def step(carry, blk):
        C, n, m = carry
        qb, kb, vb, ib, fb = blk
        b = jnp.cumsum(fb, axis=-1)
        logd = jnp.where(tri, b[..., :, None] - b[..., None, :] + ib[..., None, :], -jnp.inf)
        m_inter = b + m[..., None]
        m_t = jnp.maximum(m_inter, logd.max(-1))
        w_inter = jnp.exp(m_inter - m_t)
        s = jnp.exp(logd - m_t[..., None]) * jnp.einsum('bhtd,bhsd->bhts', qb, kb)
        num = w_inter[..., None] * jnp.einsum('bhed,bhtd->bhte', C, qb) + jnp.einsum('bhts,bhse->bhte', s, vb)
        den = w_inter * jnp.einsum('bhd,bhtd->bht', n, qb) + s.sum(-1)
        h = num / jnp.maximum(jnp.abs(den), jnp.exp(-m_t))[..., None]
        decay = b[..., -1:] - b + ib
        m_new = jnp.maximum(b[..., -1] + m, decay.max(-1))
        w_prev = jnp.exp(b[..., -1] + m - m_new)
        w_r = jnp.exp(decay - m_new[..., None])
        C = w_prev[..., None, None] * C + jnp.einsum('bhse,bhsd->bhed', w_r[..., None] * vb, kb)
        n = w_prev[..., None] * n + jnp.einsum('bhs,bhsd->bhd', w_r, kb)
        return (C, n, m_new), h

    state, hs = lax.scan(step, state, tuple(blocks(a) for a in (q, k, v, ig, lf)))
    return jnp.moveaxis(hs, 0, 2).reshape(bsz, nh, length, dh), state


def _j_mlstm_project(u, w_up, conv_w, conv_b, w_q, w_k, w_v, w_gates, b_gates):
    bsz, length, _ = u.shape
    e = w_q.shape[0]
    dh = e // ML_HEADS
    xm, z = jnp.split(u @ w_up, 2, axis=-1)
    xc = jax.nn.silu(_j_dwconv1d(xm, conv_w, conv_b))
    q = xc @ w_q
    k = (xc @ w_k) * dh ** -0.5
    v = xm @ w_v
    g = q @ w_gates[0] + k @ w_gates[1] + v @ w_gates[2] + b_gates
    heads = lambda a: a.reshape(bsz, length, ML_HEADS, dh).transpose(0, 2, 1, 3)
    return heads(q), heads(k), heads(v), g.reshape(bsz, length, 2, 2, ML_HEADS), xc, z


def _j_mlstm_out(h, xc, z, gn_g, skip, w_down):
    bsz, nh, length, dh = h.shape
    mu = h.mean(-1, keepdims=True)
    var = jnp.square(h - mu).mean(-1, keepdims=True)
    hn = ((h - mu) * lax.rsqrt(var + LN_EPS)).transpose(0, 2, 1, 3).reshape(bsz, length, nh * dh)
    hn = hn * gn_g + skip * xc
    return (hn * jax.nn.silu(z)) @ w_down


def _j_mlstm_mixer(u_lat, u_ctx, w_up, conv_w, conv_b, w_q, w_k, w_v, w_gates, b_gates, gn_g, skip, w_down):
    lat = _j_mlstm_project(u_lat, w_up, conv_w, conv_b, w_q, w_k, w_v, w_gates, b_gates)
    cxt = _j_mlstm_project(u_ctx, w_up, conv_w, conv_b, w_q, w_k, w_v, w_gates, b_gates)
    bsz = u_lat.shape[0]
    dh = w_q.shape[0] // ML_HEADS
    h_lat = jnp.zeros(lat[0].shape, F32)
    h_ctx = jnp.zeros(cxt[0].shape, F32)
    for direction in range(2):
        rev = (lambda a: jnp.flip(a, axis=2)) if direction == 1 else (lambda a: a)

        def prep(p):
            q, k, v, g = p[:4]
            ig = jnp.moveaxis(g[:, :, direction, 0], 1, 2)
            lf = jax.nn.log_sigmoid(jnp.moveaxis(g[:, :, direction, 1], 1, 2))
            return tuple(rev(a) for a in (q, k, v, ig, lf))

        st0 = (jnp.zeros((bsz, ML_HEADS, dh, dh), F32), jnp.zeros((bsz, ML_HEADS, dh), F32),
               jnp.zeros((bsz, ML_HEADS), F32))
        hc, st_ctx = _j_mlstm_scan(*prep(cxt), st0)
        hl, _ = _j_mlstm_scan(*prep(lat), st_ctx)
        h_ctx = h_ctx + rev(hc)
        h_lat = h_lat + rev(hl)
    return (_j_mlstm_out(h_lat, lat[4], lat[5], gn_g, skip, w_down),
            _j_mlstm_out(h_ctx, cxt[4], cxt[5], gn_g, skip, w_down))


def _j_conformer(u, w_in, b_in, dw_w, dw_b, ln_g, ln_b, w_out, b_out):
    a, gate = jnp.split(u @ w_in + b_in, 2, axis=-1)
    hmid = a * jax.nn.sigmoid(gate)
    hmid = jax.nn.silu(_jln(_j_dwconv1d(hmid, dw_w, dw_b), ln_g, ln_b))
    return hmid @ w_out + b_out


def _grid_transpose(x, rows, cols):
    bsz, _, ch = x.shape
    return x.reshape(bsz, rows, cols, ch).transpose(0, 2, 1, 3).reshape(bsz, rows * cols, ch)


def kernel(x, c, ctx, c_ctx, mod_w, mod_b, post_ln_g, post_ln_b, ffn_w_gate, ffn_w_up, ffn_conv_w, ffn_conv_b, ffn_w_down, s5_lambda_re, s5_lambda_im, s5_log_dt, s5_b_re, s5_b_im, s5_c_re, s5_c_im, s5_d, s5_w_glu, s5_b_glu, ml_w_up, ml_conv_w, ml_conv_b, ml_w_q, ml_w_k, ml_w_v, ml_w_gates, ml_b_gates, ml_gn_g, ml_skip, ml_w_down, cv_w_in, cv_b_in, cv_dw_w, cv_dw_b, cv_ln_g, cv_ln_b, cv_w_out, cv_b_out):
    bsz, length, d = x.shape
    depth = mod_w.shape[0]
    ctx_len = ctx.shape[1]
    rows = length // GRID_W
    alpha = (2 * depth) ** 0.25

    modv = _modulation(c, c_ctx, mod_w, mod_b)
    pad2 = jnp.zeros((2, d), F32)
    h_lat, h_ctx = x, ctx
    for i in range(depth):
        kind, occ = i % N_MIXERS, i // N_MIXERS
        last = i == depth - 1
        mod_l = jnp.concatenate([modv[i, :bsz].reshape(bsz, 6, d), jnp.zeros((bsz, 2, d), F32)], axis=1)
        mod_c = jnp.concatenate([modv[i, bsz].reshape(6, d), pad2], axis=0)[None]
        col_major = (kind != 2) and (occ % 2 == 1)
        ln1 = (post_ln_g[i, 0], post_ln_b[i, 0])
        ffn = (ffn_w_gate[i], ffn_w_up[i], ffn_conv_w[i], ffn_conv_b[i], ffn_w_down[i],
               post_ln_g[i, 1], post_ln_b[i, 1])
        if kind == 0:
            s5 = (s5_lambda_re[occ], s5_lambda_im[occ], s5_log_dt[occ], s5_b_re[occ], s5_b_im[occ],
                  s5_c_re[occ], s5_c_im[occ], s5_d[occ])
            y_lat, y_ctx = _s5_core(h_lat, h_ctx, mod_l, mod_c, s5, col_major)
            h_lat = _glu_ln(y_lat, h_lat, mod_l, s5_w_glu[occ], s5_b_glu[occ], *ln1, alpha=alpha)
            if not last:
                h_ctx = _glu_ln(y_ctx, h_ctx, mod_c, s5_w_glu[occ], s5_b_glu[occ], *ln1, alpha=alpha, mod_row=0)
        elif kind == 1:
            ml = (ml_w_up[occ], ml_conv_w[occ], ml_conv_b[occ], ml_w_q[occ], ml_w_k[occ], ml_w_v[occ],
                  ml_w_gates[occ], ml_b_gates[occ], ml_gn_g[occ], ml_skip[occ], ml_w_down[occ])
            h_in = _grid_transpose(h_lat, rows, GRID_W) if col_major else h_lat
            h_lat, h_ctx = _mlstm_layer(h_in, h_ctx, mod_l, mod_c, ml, *ln1, alpha=alpha, need_ctx=not last)
            if col_major:
                h_lat = _grid_transpose(h_lat, GRID_W, rows)
        else:
            hm_l = _cv_in(h_lat, mod_l, cv_w_in[occ], cv_b_in[occ])
            cv = (cv_dw_w[occ], cv_dw_b[occ], cv_ln_g[occ], cv_ln_b[occ], cv_w_out[occ], cv_b_out[occ])
            if not last:
                hm_c = _cv_in(h_ctx, mod_c, cv_w_in[occ], cv_b_in[occ], mod_row=0)
                h_ctx = _cv_out(hm_c, h_ctx, mod_c, *cv, *ln1, alpha=alpha, mod_row=0)
            h_lat = _cv_out(hm_l, h_lat, mod_l, *cv, *ln1, alpha=alpha)
        h_lat = _conv_ffn_ln(h_lat, mod_l, *ffn, width=GRID_W, alpha=alpha)
        if not last:
            h_ctx = _conv_ffn_ln(h_ctx, mod_c, *ffn, width=ctx_len, alpha=alpha, mod_row=0)
    return h_lat
```

```python
import functools
import math

import jax
import jax.numpy as jnp
import numpy as np
from jax import lax
from jax.experimental import pallas as pl
from jax.experimental.pallas import tpu as pltpu

F32 = jnp.float32
BF16 = jnp.bfloat16

GRID_W = 64
SCAN_BLOCK = 128
S5_GROUP = 16
S5_SUB = 16
S5_LANES = 128
ML_HEADS = 4
FFN_TILE = 1024
N_MIXERS = 3
LN_EPS = 1e-5
VMEM_LIMIT = 56 * 1024 * 1024


def _cparams(*sem):
    return pltpu.CompilerParams(dimension_semantics=sem, vmem_limit_bytes=VMEM_LIMIT)


def _const_spec(shape):
    nd = len(shape)
    return pl.BlockSpec(shape, lambda *_: (0,) * nd, pipeline_mode=pl.Buffered(1))


def _ln(x, g, b):
    mu = jnp.mean(x, axis=-1, keepdims=True)
    xc = x - mu
    var = jnp.mean(xc * xc, axis=-1, keepdims=True)
    return xc * lax.rsqrt(var + LN_EPS) * g + b


def _sigmoid(x):
    return 1.0 / (1.0 + jnp.exp(-x))


def _silu(x):
    return x * _sigmoid(x)


def _gelu(x):
    return 0.5 * x * (1.0 + jnp.tanh(math.sqrt(2.0 / math.pi) * (x + 0.044715 * (x * x * x))))


def _bdot(a, b):
    return jnp.dot(a.astype(BF16), b.astype(BF16), preferred_element_type=F32)


def _mod_kernel(c_ref, w_ref, b_ref, o_ref):
    o_ref[0] = _bdot(_silu(c_ref[...]), w_ref[0]) + b_ref[0]


def _modulation(c, c_ctx, mod_w, mod_b):
    depth, d, d6 = mod_w.shape
    bsz = c.shape[0]
    rows = 8
    cs = jnp.zeros((rows, d), F32).at[:bsz].set(c).at[bsz].set(c_ctx)
    tn = d6 // 6
    return pl.pallas_call(
        _mod_kernel,
        grid=(depth, d6 // tn),
        in_specs=[pl.BlockSpec((rows, d), lambda i, j: (0, 0)),
                  pl.BlockSpec((1, d, tn), lambda i, j: (i, 0, j)),
                  pl.BlockSpec((1, 1, tn), lambda i, j: (i, 0, j))],
        out_specs=pl.BlockSpec((1, rows, tn), lambda i, j: (i, 0, j)),
        out_shape=jax.ShapeDtypeStruct((depth, rows, d6), F32),
        compiler_params=_cparams("parallel", "parallel"),
        name="modulation",
    )(cs, mod_w, mod_b.reshape(depth, 1, d6))


def _ffn_kernel(*refs, width, tm, n_chunks, alpha, halo):
    if halo:
        top_ref, x_ref, bot_ref = refs[:3]
        refs = refs[3:]
    else:
        x_ref = refs[0]
        refs = refs[1:]
    mod_ref, wg_ref, wu_ref, cw_ref, cb_ref, wd_ref, lng_ref, lnb_ref, o_ref, xbuf, gbuf, ubuf, abuf, acc = refs
    i = pl.program_id(1)
    nt = pl.num_programs(1)
    pad = 8
    hw = width if halo else 0
    fc = gbuf.shape[-1]

    sh2 = mod_ref[0, 3:4, :]
    sc2 = 1.0 + mod_ref[0, 4:5, :]
    g2 = 1.0 + mod_ref[0, 5:6, :]
    x = x_ref[0]
    xbuf[hw:hw + tm, :] = (x * sc2 + sh2).astype(BF16)
    if halo:
        xbuf[0:hw, :] = (top_ref[0] * sc2 + sh2).astype(BF16)
        xbuf[hw + tm:, :] = (bot_ref[0] * sc2 + sh2).astype(BF16)
    for slot in range(2):
        gbuf[slot, 0:pad, :] = jnp.zeros((pad, fc), F32)
        gbuf[slot, pad + tm + 2 * hw:, :] = jnp.zeros((pad, fc), F32)
    keep_top = (i > 0).astype(F32)
    keep_bot = (i < nt - 1).astype(F32)

    col = lax.broadcasted_iota(jnp.int32, (tm, fc), 0) & (width - 1)
    not_first = col > 0
    not_last = col < width - 1

    def project(j, slot):
        g = jnp.dot(xbuf[...], wg_ref[j], preferred_element_type=F32)
        if halo:
            gbuf[slot, pad:pad + hw, :] = g[0:hw] * keep_top
            gbuf[slot, pad + hw:pad + hw + tm, :] = g[hw:hw + tm]
            gbuf[slot, pad + hw + tm:pad + 2 * hw + tm, :] = g[hw + tm:] * keep_bot
        else:
            gbuf[slot, pad:pad + tm, :] = g
        ubuf[slot] = jnp.dot(xbuf[hw:hw + tm, :], wu_ref[j], preferred_element_type=F32)

    def activate(j, slot):
        def taps(dc):
            t = None
            for dr in ((-1, 0, 1) if halo else (0,)):
                start = pad + hw + dr * width + dc
                w = cw_ref[j, 3 * (dr + 1) + dc + 1:3 * (dr + 1) + dc + 2, :]
                term = gbuf[slot, pl.ds(start, tm), :] * w
                t = term if t is None else t + term
            return t

        gate = (taps(0) + jnp.where(not_first, taps(-1), 0.0) + jnp.where(not_last, taps(1), 0.0)
                + cb_ref[j])
        abuf[slot] = (_gelu(gate) * ubuf[slot]).astype(BF16)

    def contract(j, slot):
        return jnp.dot(abuf[slot], wd_ref[j], preferred_element_type=F32)

    project(0, 0)
    if n_chunks > 1:
        project(1, 1)
    activate(0, 0)

    def step(j, slot):
        acc[...] += contract(j - 1, 1 - slot)
        project(j + 1, 1 - slot)
        activate(j, slot)

    def body(p, carry):
        step(1 + 2 * p, 1)
        step(2 + 2 * p, 0)
        return carry

    acc[...] = jnp.zeros_like(acc)
    pairs = max(n_chunks - 2, 0) // 2
    if pairs:
        lax.fori_loop(0, pairs, body, 0)
    if max(n_chunks - 2, 0) % 2:
        step(1 + 2 * pairs, 1)
    if n_chunks > 1:
        last = n_chunks - 1
        acc[...] += contract(last - 1, 1 - (last & 1))
        activate(last, last & 1)
    y = acc[...] + contract(n_chunks - 1, (n_chunks - 1) & 1)
    o_ref[0] = _ln(alpha * x + g2 * y, lng_ref[...], lnb_ref[...])


def _ffn_chunk(f):
    for fc in (256, 128):
        if f % fc == 0:
            return fc
    raise ValueError(f"ffn hidden size {f} is not a multiple of 128")


def _conv_ffn_ln(h, mod, w_gate, w_up, conv_w, conv_b, w_down, ln_g, ln_b, *, width, alpha, mod_row=None):
    bsz, length, d = h.shape
    f = w_gate.shape[1]
    fc = _ffn_chunk(f)
    nf = f // fc
    rows = length // width
    halo = rows > 1
    tm = min(length, FFN_TILE) if halo else length
    assert length % tm == 0 and tm % width == 0 and width & (width - 1) == 0
    nt = length // tm
    r = tm // width
    wg = w_gate.astype(BF16).reshape(d, nf, fc).transpose(1, 0, 2)
    wu = w_up.astype(BF16).reshape(d, nf, fc).transpose(1, 0, 2)
    wd = w_down.astype(BF16).reshape(nf, fc, d)
    cw = conv_w.reshape(9, nf, fc).transpose(1, 0, 2)
    cb = conv_b.reshape(nf, 1, fc)
    mod_map = (lambda b, i: (b, 0, 0)) if mod_row is None else (lambda b, i: (mod_row, 0, 0))
    x_spec = pl.BlockSpec((1, tm, d), lambda b, i: (b, i, 0))
    in_specs, args = [x_spec], [h]
    if halo:
        nrow = length // width
        in_specs = [pl.BlockSpec((1, width, d), lambda b, i: (b, jnp.maximum(i * r - 1, 0), 0)), x_spec,
                    pl.BlockSpec((1, width, d), lambda b, i: (b, jnp.minimum((i + 1) * r, nrow - 1), 0))]
        args = [h, h, h]
    in_specs += [pl.BlockSpec((1, 8, d), mod_map), _const_spec(wg.shape), _const_spec(wu.shape),
                 _const_spec(cw.shape), _const_spec(cb.shape), _const_spec(wd.shape),
                 _const_spec((1, d)), _const_spec((1, d))]
    args += [mod, wg, wu, cw, cb, wd, ln_g.reshape(1, d), ln_b.reshape(1, d)]
    hw = width if halo else 0
    return pl.pallas_call(
        functools.partial(_ffn_kernel, width=width, tm=tm, n_chunks=nf, alpha=alpha, halo=halo),
        grid=(bsz, nt),
        in_specs=in_specs,
        out_specs=pl.BlockSpec((1, tm, d), lambda b, i: (b, i, 0)),
        out_shape=jax.ShapeDtypeStruct((bsz, length, d), F32),
        scratch_shapes=[pltpu.VMEM((tm + 2 * hw, d), BF16),
                        pltpu.VMEM((2, tm + 2 * hw + 16, fc), F32),
                        pltpu.VMEM((2, tm, fc), F32),
                        pltpu.VMEM((2, tm, fc), BF16),
                        pltpu.VMEM((tm, d), F32)],
        compiler_params=_cparams("parallel", "parallel"),
        name="conv_ffn_ln",
    )(*args)


def _s5_operators(lam_re, lam_im, log_dt, b_re, b_im, c_re, c_im):
    hi = lax.Precision.HIGHEST
    s = S5_SUB
    e_cols, f_rows, m_tot, a_rows = [], [], None, []
    for direction in range(2):
        lre = jnp.minimum(lam_re[direction], -1e-4)
        lim = lam_im[direction]
        dt = jnp.exp(log_dt[direction])[:, None]
        mag = jnp.exp(lre * dt)
        ar, ai = mag * jnp.cos(lim * dt), mag * jnp.sin(lim * dt)
        nr, ni = ar - 1.0, ai
        den = lre * lre + lim * lim
        cr = (nr * lre + ni * lim) / den
        ci = (ni * lre - nr * lim) / den
        bbr = cr[..., None] * b_re[direction] - ci[..., None] * b_im[direction]
        bbi = cr[..., None] * b_im[direction] + ci[..., None] * b_re[direction]
        pr, pi = [jnp.ones_like(ar)], [jnp.zeros_like(ai)]
        for _ in range(s):
            pr.append(pr[-1] * ar - pi[-1] * ai)
            pi.append(pr[-2] * ai + pi[-1] * ar)
        pr, pi = jnp.stack(pr), jnp.stack(pi)
        abr = pr[:s, :, :, None] * bbr - pi[:s, :, :, None] * bbi
        abi = pr[:s, :, :, None] * bbi + pi[:s, :, :, None] * bbr
        ccr, cci = c_re[direction], c_im[direction]
        kern = (jnp.einsum('gop,tgpi->tgoi', ccr, abr, precision=hi)
                - jnp.einsum('gop,tgpi->tgoi', cci, abi, precision=hi))
        j = np.arange(s)[:, None]
        t = np.arange(s)[None, :]
        lag = (t - j) if direction == 0 else (j - t)
        place = jnp.asarray(lag[:, :, None] == np.arange(s), F32)
        m = jnp.einsum('jtz,zgoi->gjito', place, kern, precision=hi)
        m_tot = m if m_tot is None else m_tot + m
        flip = (lambda a: a[::-1]) if direction == 0 else (lambda a: a)
        er = flip(abr).transpose(1, 0, 3, 2)
        ei = flip(abi).transpose(1, 0, 3, 2)
        e_cols.append((er, ei))
        zr, zi = (pr[1:], pi[1:]) if direction == 0 else (pr[:0:-1], pi[:0:-1])
        f_re = jnp.einsum('gop,tgp->gpto', ccr, zr) - jnp.einsum('gop,tgp->gpto', cci, zi)
        f_im = -(jnp.einsum('gop,tgp->gpto', ccr, zi) + jnp.einsum('gop,tgp->gpto', cci, zr))
        f_rows.append((f_re, f_im))
        a_rows.append((jnp.concatenate([pr[s], pr[s]], -1), jnp.concatenate([-pi[s], pi[s]], -1)))
    g = lam_re.shape[1]
    w = s * S5_GROUP
    (fer, fei), (ber, bei) = e_cols
    e_mat = jnp.concatenate([fer, fei, ber, bei, fei, fer, bei, ber], axis=-1).reshape(g, w, -1)
    (ffr, ffi), (bfr, bfi) = f_rows
    f_mat = jnp.concatenate([ffr, ffi, bfr, bfi], axis=1).reshape(g, -1, w)
    a_mat = jnp.stack([a_rows[0][0], a_rows[0][1], a_rows[1][0], a_rows[1][1]], axis=1)
    return e_mat, m_tot.reshape(g, w, w), f_mat, a_mat


def _pick_group(res, n, gp, bsz):
    if gp == 1:
        return res
    sel = (lax.broadcasted_iota(jnp.int32, (res.shape[0], n), 0) & 7) // bsz
    out = res[:, :n]
    for k in range(1, gp):
        out = jnp.where(sel == k, res[:, k * n:(k + 1) * n], out)
    return out


def _s5_kernel(xc_ref, xl_ref, d_ref, a_ref, e_ref, m_ref, f_ref, yc_ref, yl_ref, se, *, bsz, gp, rc):
    hv = xc_ref.shape[1]
    rows_c, w = xc_ref.shape[2], hv * xc_ref.shape[3]
    rows_l = xl_ref.shape[2]
    ln = w // hv

    def rows_of(x_ref, r0, n):
        return jnp.concatenate([x_ref[0, hf, pl.ds(r0, n), :] for hf in range(hv)], axis=1)
    p2 = a_ref.shape[-1]

    def chunks(rows):
        return [(r0, min(rc, rows - r0)) for r0 in range(0, rows, rc)]

    segments = ((xc_ref, yc_ref, 0, rows_c), (xl_ref, yl_ref, rows_c, rows_l))

    for x_ref, _, base, rows in segments:
        for r0, n in chunks(rows):
            u = rows_of(x_ref, r0, n).astype(BF16)
            e = jnp.dot(u, e_ref[0], preferred_element_type=F32)
            se[base + r0:base + r0 + n, :] = _pick_group(e, 4 * p2, gp, bsz)

    a1f, a2f, a1b, a2b = a_ref[0, 0], a_ref[0, 1], a_ref[0, 2], a_ref[0, 3]

    def make_step(base, tiles):
        def step(s, carry):
            vf, wf, vb, wb = carry
            rf = pl.multiple_of(base + s * 8, 8)
            rb = pl.multiple_of(base + (tiles - 1 - s) * 8, 8)
            ef_v = se[pl.ds(rf, 8), 0:p2]
            ef_w = se[pl.ds(rf, 8), 2 * p2:3 * p2]
            eb_v = se[pl.ds(rb, 8), p2:2 * p2]
            eb_w = se[pl.ds(rb, 8), 3 * p2:4 * p2]
            se[pl.ds(rf, 8), 0:p2] = vf
            se[pl.ds(rb, 8), p2:2 * p2] = vb
            return (a1f * vf + a2f * wf + ef_v, a1f * wf - a2f * vf + ef_w,
                    a1b * vb + a2b * wb + eb_v, a1b * wb - a2b * vb + eb_w)
        return step

    zero = jnp.zeros((8, p2), F32)
    carry = (zero, zero, zero, zero)
    carry = lax.fori_loop(0, rows_c // 8, make_step(0, rows_c // 8), carry, unroll=2)
    lax.fori_loop(0, rows_l // 8, make_step(rows_c, rows_l // 8), carry, unroll=2)

    for x_ref, y_ref, base, rows in segments:
        for r0, n in chunks(rows):
            u = rows_of(x_ref, r0, n)
            s_in = se[base + r0:base + r0 + n, 0:2 * p2].astype(BF16)
            y = (_pick_group(jnp.dot(u.astype(BF16), m_ref[0], preferred_element_type=F32), w, gp, bsz)
                 + _pick_group(jnp.dot(s_in, f_ref[0], preferred_element_type=F32), w, gp, bsz))
            y = y + (u.reshape(n // 8, 8, w) * d_ref[0]).reshape(n, w)
            y = _gelu(y)
            for hf in range(hv):
                y_ref[0, hf, pl.ds(r0, n), :] = y[:, hf * ln:(hf + 1) * ln]


def _chunk_butterfly(lo, hi):
    n = len(lo)
    lanes = lo[0].shape[1]
    chunk = lax.broadcasted_iota(jnp.int32, lo[0].shape, 1) // S5_GROUP
    m = 1
    while m < n // 2:
        odd = (chunk & m) != 0

        def exchange(v, m=m, odd=odd):
            out = []
            for t in range(n):
                if t & m == 0:
                    out.append(jnp.where(odd, pltpu.roll(v[t + m], S5_GROUP * m, 1), v[t]))
                else:
                    out.append(jnp.where(odd, v[t], pltpu.roll(v[t - m], lanes - S5_GROUP * m, 1)))
            return out

        lo, hi = exchange(lo), exchange(hi)
        m *= 2
    h = n // 2
    return ([lo[t] if t < h else hi[t - h] for t in range(n)],
            [lo[t + h] if t < h else hi[t] for t in range(n)])


def _s5_modulated_rows(x_refs, mod_ref, b, load):
    ln = x_refs[0].shape[-1]
    mb = min(b, mod_ref.shape[0] - 1)
    halves = []
    for hf, x_ref in enumerate(x_refs):
        sc = 1.0 + mod_ref[mb, 1:2, hf * ln:(hf + 1) * ln]
        sh = mod_ref[mb, 0:1, hf * ln:(hf + 1) * ln]
        halves.append([load(x_ref, t) * sc + sh for t in range(S5_SUB)])
    return halves


def _s5_pack_kernel(xa_ref, xb_ref, mod_ref, o_ref, *, bsz, gp):
    n = S5_SUB

    def body(i, carry):
        for b in range(bsz):
            lo, hi = _chunk_butterfly(*_s5_modulated_rows(
                (xa_ref, xb_ref), mod_ref, b, lambda x_ref, t: x_ref[b, pl.ds(i * 8 * n + t, 8, stride=n), :]))
            for g in range(n):
                tile, gs = divmod(g, gp)
                rows = pl.ds(i * 64 + gs * bsz + b, 8, stride=8)
                o_ref[tile, 0, rows, :] = lo[g]
                o_ref[tile, 1, rows, :] = hi[g]
        return carry

    lax.fori_loop(0, xa_ref.shape[1] // (8 * n), body, 0)


def _s5_pack_cm_kernel(xa_ref, xb_ref, mod_ref, o_ref, *, bsz, gp):
    n = S5_SUB
    nrk = xa_ref.shape[1] // n

    def body(rk, carry):
        for b in range(bsz):
            lo, hi = _chunk_butterfly(*_s5_modulated_rows(
                (xa_ref, xb_ref), mod_ref, b, lambda x_ref, t: x_ref[b, rk * n + t]))
            for g in range(n):
                tile, gs = divmod(g, gp)
                rows = pl.ds(rk * 8 + gs * bsz + b, 8, stride=nrk * 8)
                o_ref[tile, 0, rows, :] = lo[g]
                o_ref[tile, 1, rows, :] = hi[g]
        return carry

    lax.fori_loop(0, nrk, body, 0)


def _s5_unpack_kernel(y_ref, o_ref, obuf, *, bsz, gp):
    n = S5_SUB

    def body(i, carry):
        for b in range(bsz):
            rows = lambda g: pl.ds(i * 64 + (g % gp) * bsz + b, 8, stride=8)
            lo, hi = _chunk_butterfly([y_ref[g // gp, 0, rows(g), :] for g in range(n)],
                                      [y_ref[g // gp, 1, rows(g), :] for g in range(n)])
            for t in range(n):
                tok = pl.ds(i * 8 * n + t, 8, stride=n)
                obuf[b, 0, tok, :] = lo[t]
                obuf[b, 1, tok, :] = hi[t]
        return carry

    lax.fori_loop(0, o_ref.shape[1] // (8 * n), body, 0)
    for b in range(bsz):
        o_ref[b] = jnp.concatenate([obuf[b, 0], obuf[b, 1]], axis=1).astype(o_ref.dtype)


def _s5_unpack_cm_kernel(y_ref, o_ref, *, bsz, gp):
    n = S5_SUB
    nrk = o_ref.shape[1] // n
    ln = y_ref.shape[-1]

    def body(rk, carry):
        for b in range(bsz):
            rows = lambda g: pl.ds(rk * 8 + (g % gp) * bsz + b, 8, stride=nrk * 8)
            lo, hi = _chunk_butterfly([y_ref[g // gp, 0, rows(g), :] for g in range(n)],
                                      [y_ref[g // gp, 1, rows(g), :] for g in range(n)])
            for t in range(n):
                o_ref[b, rk * n + t, :, 0:ln] = lo[t]
                o_ref[b, rk * n + t, :, ln:2 * ln] = hi[t]
        return carry

    lax.fori_loop(0, nrk, body, 0)


def _s5_pack(h, mod, gp, grid_rows=None):
    bsz, length, d = h.shape
    w = S5_SUB * S5_GROUP
    ln = S5_LANES
    assert S5_SUB == 16 and S5_GROUP == 16 and w == 2 * ln and d % w == 0
    tiles = S5_SUB // gp
    out_shape = jax.ShapeDtypeStruct((d // S5_GROUP // gp, 2, length // S5_SUB * 8, ln), F32)
    mod_spec = pl.BlockSpec((mod.shape[0], 8, w), lambda q, i: (0, 0, q))
    if grid_rows is None:
        tb = min(length, 512)
        return pl.pallas_call(
            functools.partial(_s5_pack_kernel, bsz=bsz, gp=gp),
            grid=(d // w, length // tb),
            in_specs=[pl.BlockSpec((bsz, tb, ln), lambda q, i: (0, i, 2 * q)),
                      pl.BlockSpec((bsz, tb, ln), lambda q, i: (0, i, 2 * q + 1)), mod_spec],
            out_specs=pl.BlockSpec((tiles, 2, tb // S5_SUB * 8, ln), lambda q, i: (q, 0, i, 0)),
            out_shape=out_shape,
            compiler_params=_cparams("parallel", "parallel"),
            name="s5_pack",
        )(h, h, mod)
    cols = length // grid_rows
    assert grid_rows % S5_SUB == 0 and cols % 8 == 0
    h4 = h.reshape(bsz, grid_rows, cols, d)
    return pl.pallas_call(
        functools.partial(_s5_pack_cm_kernel, bsz=bsz, gp=gp),
        grid=(d // w, cols // 8),
        in_specs=[pl.BlockSpec((bsz, grid_rows, 8, ln), lambda q, i: (0, 0, i, 2 * q)),
                  pl.BlockSpec((bsz, grid_rows, 8, ln), lambda q, i: (0, 0, i, 2 * q + 1)), mod_spec],
        out_specs=pl.BlockSpec((tiles, 2, grid_rows // 2 * 8, ln), lambda q, i: (q, 0, i, 0)),
        out_shape=out_shape,
        compiler_params=_cparams("parallel", "parallel"),
        name="s5_pack_cm",
    )(h4, h4, mod)


def _s5_unpack(y, bsz, gp, grid_rows=None):
    gt, hv, rows, ln = y.shape
    w = hv * ln
    length = rows // 8 * S5_SUB
    d = gt * gp * S5_GROUP
    tiles = S5_SUB // gp
    if grid_rows is None:
        tb = min(length, 512)
        return pl.pallas_call(
            functools.partial(_s5_unpack_kernel, bsz=bsz, gp=gp),
            grid=(d // w, length // tb),
            in_specs=[pl.BlockSpec((tiles, hv, tb // S5_SUB * 8, ln), lambda q, i: (q, 0, i, 0))],
            out_specs=pl.BlockSpec((bsz, tb, w), lambda q, i: (0, i, q)),
            out_shape=jax.ShapeDtypeStruct((bsz, length, d), BF16),
            scratch_shapes=[pltpu.VMEM((bsz, hv, tb, ln), F32)],
            compiler_params=_cparams("parallel", "parallel"),
            name="s5_unpack",
        )(y)
    cols = length // grid_rows
    out = pl.pallas_call(
        functools.partial(_s5_unpack_cm_kernel, bsz=bsz, gp=gp),
        grid=(d // w, cols // 8),
        in_specs=[pl.BlockSpec((tiles, hv, grid_rows // 2 * 8, ln), lambda q, i: (q, 0, i, 0))],
        out_specs=pl.BlockSpec((bsz, grid_rows, 8, w), lambda q, i: (0, 0, i, q)),
        out_shape=jax.ShapeDtypeStruct((bsz, grid_rows, cols, d), F32),
        compiler_params=_cparams("parallel", "parallel"),
        name="s5_unpack_cm",
    )(y)
    return out.reshape(bsz, length, d)


def _s5_pattern(v, gp, reps):
    d = v.shape[-1]
    gt = d // S5_GROUP // gp
    x = jnp.broadcast_to(v.reshape(gt, gp, 1, 1, S5_GROUP), (gt, gp, reps, S5_SUB, S5_GROUP))
    return x.reshape(gt, gp * reps, S5_SUB * S5_GROUP)


def _grid_transpose(x, rows, cols):
    bsz, _, ch = x.shape
    return x.reshape(bsz, rows, cols, ch).transpose(0, 2, 1, 3).reshape(bsz, rows * cols, ch)


def _s5_core(h_lat, h_ctx, mod_l, mod_c, params, col_major):
    lam_re, lam_im, log_dt, b_re, b_im, c_re, c_im, d_skip = params
    bsz, length, d = h_lat.shape
    assert 8 % bsz == 0 and S5_SUB % (8 // bsz) == 0
    gp = 8 // bsz
    gt = d // S5_GROUP // gp
    w = S5_SUB * S5_GROUP
    grid_rows = (length // GRID_W) if col_major else None
    xl = _s5_pack(h_lat, mod_l, gp, grid_rows)
    xc = _s5_pack(h_ctx, mod_c, gp)
    e_mat, m_mat, f_mat, a_mat = _s5_operators(lam_re, lam_im, log_dt, b_re, b_im, c_re, c_im)
    p2 = a_mat.shape[-1]
    cat = lambda m: m.reshape(gt, gp, m.shape[1], m.shape[2]).transpose(0, 2, 1, 3).reshape(gt, m.shape[1], -1)
    e_cat, m_cat, f_cat = (cat(m).astype(BF16) for m in (e_mat, m_mat, f_mat))
    a_pat = jnp.broadcast_to(a_mat.reshape(gt, gp, 1, 4, p2), (gt, gp, bsz, 4, p2)).transpose(0, 3, 1, 2, 4)
    a_pat = a_pat.reshape(gt, 4, 8, p2)
    d_pat = _s5_pattern(d_skip, gp, bsz)
    hv, rows_l, rows_c, ln = xl.shape[1], xl.shape[2], xc.shape[2], xl.shape[3]
    rc = 512
    tile = lambda *shape: pl.BlockSpec((1,) + shape, lambda g: (g,) + (0,) * len(shape))
    yc, yl = pl.pallas_call(
        functools.partial(_s5_kernel, bsz=bsz, gp=gp, rc=rc),
        grid=(gt,),
        in_specs=[tile(hv, rows_c, ln), tile(hv, rows_l, ln), tile(8, w), tile(4, 8, p2),
                  tile(w, gp * 4 * p2), tile(w, gp * w), tile(2 * p2, gp * w)],
        out_specs=[tile(hv, rows_c, ln), tile(hv, rows_l, ln)],
        out_shape=[jax.ShapeDtypeStruct((gt, hv, rows_c, ln), F32),
                   jax.ShapeDtypeStruct((gt, hv, rows_l, ln), F32)],
        scratch_shapes=[pltpu.VMEM((rows_c + rows_l, 4 * p2), F32)],
        compiler_params=_cparams("parallel"),
        name="s5_scan",
    )(xc, xl, d_pat, a_pat, e_cat, m_cat, f_cat)
    return _s5_unpack(yl, bsz, gp, grid_rows), _s5_unpack(yc, bsz, gp)


def _glu_ln_kernel(y_ref, h_ref, mod_ref, w_ref, b_ref, lng_ref, lnb_ref, o_ref, *, alpha):
    d = h_ref.shape[-1]
    z = jnp.dot(y_ref[0].astype(BF16), w_ref[...], preferred_element_type=F32) + b_ref[...]
    glu = z[:, :d] * _sigmoid(z[:, d:])
    g1 = 1.0 + mod_ref[0, 2:3, :]
    o_ref[0] = _ln(alpha * h_ref[0] + g1 * glu, lng_ref[...], lnb_ref[...])


def _glu_ln(y, h, mod, w_glu, b_glu, ln_g, ln_b, *, alpha, mod_row=None):
    bsz, length, d = h.shape
    tm = min(length, 512)
    mod_map = (lambda b, i: (b, 0, 0)) if mod_row is None else (lambda b, i: (mod_row, 0, 0))
    tok = lambda: pl.BlockSpec((1, tm, d), lambda b, i: (b, i, 0))
    return pl.pallas_call(
        functools.partial(_glu_ln_kernel, alpha=alpha),
        grid=(bsz, length // tm),
        in_specs=[tok(), tok(), pl.BlockSpec((1, 8, d), mod_map), _const_spec((d, 2 * d)), _const_spec((1, 2 * d)),
                  _const_spec((1, d)), _const_spec((1, d))],
        out_specs=tok(),
        out_shape=jax.ShapeDtypeStruct((bsz, length, d), F32),
        compiler_params=_cparams("parallel", "parallel"),
        name="glu_ln",
    )(y, h, mod, w_glu.astype(BF16), b_glu.reshape(1, 2 * d), ln_g.reshape(1, d), ln_b.reshape(1, d))


def _mod_map(mod_row):
    return (lambda b, i: (b, 0, 0)) if mod_row is None else (lambda b, i: (mod_row, 0, 0))


def _ml_up_kernel(h_ref, mod_ref, w_ref, xm_ref, z_ref):
    e = xm_ref.shape[-1]
    u = (h_ref[0] * (1.0 + mod_ref[0, 1:2, :]) + mod_ref[0, 0:1, :]).astype(BF16)
    y = jnp.dot(u, w_ref[...], preferred_element_type=F32)
    xm_ref[0] = y[:, :e]
    z_ref[0] = y[:, e:].astype(BF16)


def _ml_up(h, mod, w_up, *, mod_row=None):
    bsz, length, d = h.shape
    e = w_up.shape[1] // 2
    tm = min(length, 512)
    tok = lambda n: pl.BlockSpec((1, tm, n), lambda b, i: (b, i, 0))
    return pl.pallas_call(
        _ml_up_kernel,
        grid=(bsz, length // tm),
        in_specs=[tok(d), pl.BlockSpec((1, 8, d), _mod_map(mod_row)), _const_spec((d, 2 * e))],
        out_specs=[tok(e), tok(e)],
        out_shape=[jax.ShapeDtypeStruct((bsz, length, e), F32), jax.ShapeDtypeStruct((bsz, length, e), BF16)],
        compiler_params=_cparams("parallel", "parallel"),
        name="ml_up",
    )(h, mod, w_up.astype(BF16))


def _ml_qkv_kernel(prev_ref, x_ref, next_ref, cw_ref, cb_ref, w_ref, wg_ref, bg_ref,
                   xc_ref, q_ref, k_ref, v_ref, g_ref, xbuf, *, k_scale):
    i = pl.program_id(1)
    nt = pl.num_programs(1)
    tm, e = x_ref.shape[1], x_ref.shape[2]
    hp = prev_ref.shape[1]
    xbuf[0:hp, :] = jnp.where(i > 0, prev_ref[0], 0.0)
    xbuf[hp:hp + tm, :] = x_ref[0]
    xbuf[hp + tm:, :] = jnp.where(i < nt - 1, next_ref[0], 0.0)
    x = x_ref[0]
    conv = (cw_ref[0:1, :] * xbuf[hp - 1:hp - 1 + tm, :] + cw_ref[1:2, :] * x
            + cw_ref[2:3, :] * xbuf[hp + 1:hp + 1 + tm, :] + cb_ref[...])
    xc = _silu(conv).astype(BF16)
    xc_ref[0] = xc
    q = jnp.dot(xc, w_ref[0], preferred_element_type=F32).astype(BF16)
    k = (jnp.dot(xc, w_ref[1], preferred_element_type=F32) * k_scale).astype(BF16)
    v = jnp.dot(x.astype(BF16), w_ref[2], preferred_element_type=F32).astype(BF16)
    q_ref[0] = q
    k_ref[0] = k
    v_ref[0] = v
    g_ref[0] = (jnp.dot(q, wg_ref[0], preferred_element_type=F32) + jnp.dot(k, wg_ref[1], preferred_element_type=F32)
                + jnp.dot(v, wg_ref[2], preferred_element_type=F32) + bg_ref[...])


def _ml_qkv(xm, conv_w, conv_b, w_q, w_k, w_v, w_gates, b_gates):
    bsz, length, e = xm.shape
    tm = min(length, 256)
    hp = 8
    r = tm // hp
    nb = length // hp
    ng = w_gates.shape[-1]
    lanes = 128
    w3 = jnp.stack([w_q, w_k, w_v]).astype(BF16)
    wg = jnp.zeros((3, e, lanes), BF16).at[:, :, :ng].set(w_gates.astype(BF16))
    bg = jnp.zeros((1, lanes), F32).at[0, :ng].set(b_gates)
    cw = jnp.zeros((8, e), F32).at[:conv_w.shape[0]].set(conv_w)
    tok = lambda n: pl.BlockSpec((1, tm, n), lambda b, i: (b, i, 0))
    return pl.pallas_call(
        functools.partial(_ml_qkv_kernel, k_scale=(e // ML_HEADS) ** -0.5),
        grid=(bsz, length // tm),
        in_specs=[pl.BlockSpec((1, hp, e), lambda b, i: (b, jnp.maximum(i * r - 1, 0), 0)), tok(e),
                  pl.BlockSpec((1, hp, e), lambda b, i: (b, jnp.minimum((i + 1) * r, nb - 1), 0)),
                  _const_spec((8, e)), _const_spec((1, e)), _const_spec((3, e, e)), _const_spec((3, e, lanes)),
                  _const_spec((1, lanes))],
        out_specs=[tok(e), tok(e), tok(e), tok(e), tok(lanes)],
        out_shape=[jax.ShapeDtypeStruct((bsz, length, e), BF16)] * 4
                  + [jax.ShapeDtypeStruct((bsz, length, lanes), F32)],
        scratch_shapes=[pltpu.VMEM((tm + 2 * hp, e), F32)],
        compiler_params=_cparams("parallel", "parallel"),
        name="ml_qkv",
    )(xm, xm, xm, cw, conv_b.reshape(1, e), w3, wg, bg)


def _log_sigmoid(x):
    return jnp.minimum(x, 0.0) - jnp.log(1.0 + jnp.exp(-jnp.abs(x)))


def _ml_scan_kernel(*refs, nblk_c, heads):
    (qcf, kcf, vcf, gccf, grcf, qcb, kcb, vcb, gccb, grcb,
     qlf, klf, vlf, gclf, grlf, qlb, klb, vlb, gclb, grlb,
     hf_ref, hb_ref, c_sc, cb_sc, n_sc, m_sc) = refs
    i = pl.program_id(1)
    is_ctx = i < nblk_c
    t = qcf.shape[1]
    e = qcf.shape[2]
    dh = e // heads

    @pl.when(i == 0)
    def _():
        c_sc[...] = jnp.zeros_like(c_sc)
        cb_sc[...] = jnp.zeros_like(cb_sc)
        n_sc[...] = jnp.zeros_like(n_sc)
        m_sc[...] = jnp.zeros_like(m_sc)

    row = lax.broadcasted_iota(jnp.int32, (t, t), 0)
    col = lax.broadcasted_iota(jnp.int32, (t, t), 1)
    pick = lambda c_ref, l_ref: jnp.where(is_ctx, c_ref[0], l_ref[0])
    for direction, blk in enumerate(((qcf, kcf, vcf, gccf, grcf, qlf, klf, vlf, gclf, grlf),
                                     (qcb, kcb, vcb, gccb, grcb, qlb, klb, vlb, gclb, grlb))):
        q_all, k_all, v_all, gc, gr = (pick(blk[j], blk[j + 5]) for j in range(5))
        mask = (col <= row) if direction == 0 else (col >= row)
        mask_t = (row <= col) if direction == 0 else (row >= col)
        h_ref = hf_ref if direction == 0 else hb_ref
        for head in range(heads):
            r = direction * heads + head
            ci = direction * 2 * heads + head
            cf = ci + heads
            q = q_all[:, head * dh:(head + 1) * dh]
            k = k_all[:, head * dh:(head + 1) * dh]
            v = v_all[:, head * dh:(head + 1) * dh]
            ig_col, ig_row = gc[:, ci:ci + 1], gr[ci:ci + 1, :]
            lf_col, lf_row = _log_sigmoid(gc[:, cf:cf + 1]), _log_sigmoid(gr[cf:cf + 1, :])
            m_prev = m_sc[r, :, 0:1]
            b_col = jnp.sum(jnp.where(mask, jnp.broadcast_to(lf_row, (t, t)), 0.0), axis=1, keepdims=True)
            b_row = jnp.sum(jnp.where(mask_t, jnp.broadcast_to(lf_col, (t, t)), 0.0), axis=0, keepdims=True)
            b_tot = jnp.sum(lf_row, axis=1, keepdims=True)
            logd = jnp.where(mask, b_col - b_row + ig_row, -jnp.inf)
            m_inter = b_col + m_prev
            m_t = jnp.maximum(m_inter, jnp.max(logd, axis=1, keepdims=True))
            w_inter = jnp.exp(m_inter - m_t)
            qk = lax.dot_general(q, k, (((1,), (1,)), ((), ())), preferred_element_type=F32)
            s = jnp.exp(logd - m_t) * qk
            n_row = n_sc[r]
            num = (w_inter * jnp.dot(q, cb_sc[r], preferred_element_type=F32)
                   + jnp.dot(s.astype(BF16), v, preferred_element_type=F32))
            den = (w_inter * jnp.sum(q.astype(F32) * n_row, axis=1, keepdims=True)
                   + jnp.sum(s, axis=1, keepdims=True))
            h = num / jnp.maximum(jnp.abs(den), jnp.exp(-m_t))
            h_ref[0, :, head * dh:(head + 1) * dh] = h.astype(h_ref.dtype)
            decay = b_tot - b_col + ig_col
            m_new = jnp.maximum(b_tot + m_prev, jnp.max(decay, axis=0, keepdims=True))
            w_prev = jnp.exp(b_tot + m_prev - m_new)
            w_r = jnp.exp(decay - m_new)
            wv = (w_r * v.astype(F32)).astype(BF16)
            c_new = w_prev * c_sc[r] + lax.dot_general(k, wv, (((0,), (0,)), ((), ())), preferred_element_type=F32)
            c_sc[r] = c_new
            cb_sc[r] = c_new.astype(BF16)
            n_sc[r] = w_prev * n_row + jnp.sum(w_r * k.astype(F32), axis=0, keepdims=True)
            m_sc[r] = jnp.broadcast_to(m_new, m_sc.shape[1:])


def _ml_scan(qkv_c, g_c, qkv_l, g_l):
    bsz, len_c, e = qkv_c[0].shape
    len_l = qkv_l[0].shape[1]
    t = SCAN_BLOCK
    nc, nl = len_c // t, len_l // t
    lanes = g_c.shape[-1]
    rows = 16
    grow = lambda g: jnp.swapaxes(g[:, :, :rows], 1, 2)
    cf = lambda i: jnp.minimum(i, nc - 1)
    cb = lambda i: jnp.maximum(nc - 1 - i, 0)
    lf = lambda i: jnp.maximum(i - nc, 0)
    lb = lambda i: jnp.minimum(nl - 1 - (i - nc), nl - 1)
    def specs(idx):
        tok = pl.BlockSpec((1, t, e), lambda b, i: (b, idx(i), 0))
        return [tok, tok, tok, pl.BlockSpec((1, t, lanes), lambda b, i: (b, idx(i), 0)),
                pl.BlockSpec((1, rows, t), lambda b, i: (b, 0, idx(i)))]
    args_c = list(qkv_c) + [g_c, grow(g_c)]
    args_l = list(qkv_l) + [g_l, grow(g_l)]
    fwd = lambda i: i
    bwd = lambda i: jnp.where(i < nc, nc - 1 - i, 2 * nc + nl - 1 - i)
    out = lambda idx: pl.BlockSpec((1, t, e), lambda b, i: (b, idx(i), 0))
    heads = ML_HEADS
    dh = e // heads
    return pl.pallas_call(
        functools.partial(_ml_scan_kernel, nblk_c=nc, heads=heads),
        grid=(bsz, nc + nl),
        in_specs=specs(cf) + specs(cb) + specs(lf) + specs(lb),
        out_specs=[out(fwd), out(bwd)],
        out_shape=[jax.ShapeDtypeStruct((bsz, len_c + len_l, e), BF16)] * 2,
        scratch_shapes=[pltpu.VMEM((2 * heads, dh, dh), F32), pltpu.VMEM((2 * heads, dh, dh), BF16),
                        pltpu.VMEM((2 * heads, 1, dh), F32), pltpu.VMEM((2 * heads, 1, 128), F32)],
        compiler_params=_cparams("parallel", "arbitrary"),
        name="ml_scan",
    )(*(args_c + args_c + args_l + args_l))


def _ml_out_kernel(hf_ref, hb_ref, xc_ref, z_ref, h_ref, mod_ref, gn_ref, sk_ref, w_ref, lng_ref, lnb_ref, o_ref,
                   *, alpha, heads):
    e = hf_ref.shape[-1]
    dh = e // heads
    hh = hf_ref[0].astype(F32) + hb_ref[0].astype(F32)
    parts = []
    for head in range(heads):
        x = hh[:, head * dh:(head + 1) * dh]
        mu = jnp.mean(x, axis=-1, keepdims=True)
        xc = x - mu
        var = jnp.mean(xc * xc, axis=-1, keepdims=True)
        parts.append(xc * lax.rsqrt(var + LN_EPS))
    hn = jnp.concatenate(parts, axis=1) * gn_ref[...] + sk_ref[...] * xc_ref[0].astype(F32)
    a = (hn * _silu(z_ref[0].astype(F32))).astype(BF16)
    y = jnp.dot(a, w_ref[...], preferred_element_type=F32)
    g1 = 1.0 + mod_ref[0, 2:3, :]
    o_ref[0] = _ln(alpha * h_ref[0] + g1 * y, lng_ref[...], lnb_ref[...])


def _ml_out(hf, hb, xc, z, h, mod, gn_g, skip, w_down, ln_g, ln_b, *, alpha, mod_row=None, offset=0):
    bsz, length, d = h.shape
    e = hf.shape[-1]
    tm = min(length, 256)
    assert offset % tm == 0
    tok = lambda n: pl.BlockSpec((1, tm, n), lambda b, i: (b, i, 0))
    hid = pl.BlockSpec((1, tm, e), lambda b, i: (b, i + offset // tm, 0))
    return pl.pallas_call(
        functools.partial(_ml_out_kernel, alpha=alpha, heads=ML_HEADS),
        grid=(bsz, length // tm),
        in_specs=[hid, hid, tok(e), tok(e), tok(d), pl.BlockSpec((1, 8, d), _mod_map(mod_row)),
                  _const_spec((1, e)), _const_spec((1, e)), _const_spec((e, d)), _const_spec((1, d)),
                  _const_spec((1, d))],
        out_specs=tok(d),
        out_shape=jax.ShapeDtypeStruct((bsz, length, d), F32),
        compiler_params=_cparams("parallel", "parallel"),
        name="ml_out",
    )(hf, hb, xc, z, h, mod, gn_g.reshape(1, e), skip.reshape(1, e), w_down.astype(BF16),
      ln_g.reshape(1, d), ln_b.reshape(1, d))


def _mlstm_layer(h_lat, h_ctx, mod_l, mod_c, p, ln_g, ln_b, *, alpha, need_ctx):
    (w_up, conv_w, conv_b, w_q, w_k, w_v, w_gates, b_gates, gn_g, skip, w_down) = p
    xm_l, z_l = _ml_up(h_lat, mod_l, w_up)
    xm_c, z_c = _ml_up(h_ctx, mod_c, w_up, mod_row=0)
    xc_l, q_l, k_l, v_l, g_l = _ml_qkv(xm_l, conv_w, conv_b, w_q, w_k, w_v, w_gates, b_gates)
    xc_c, q_c, k_c, v_c, g_c = _ml_qkv(xm_c, conv_w, conv_b, w_q, w_k, w_v, w_gates, b_gates)
    hf, hb = _ml_scan((q_c, k_c, v_c), g_c, (q_l, k_l, v_l), g_l)
    out_l = _ml_out(hf, hb, xc_l, z_l, h_lat, mod_l, gn_g, skip, w_down, ln_g, ln_b, alpha=alpha,
                    offset=h_ctx.shape[1])
    out_c = None
    if need_ctx:
        out_c = _ml_out(hf, hb, xc_c, z_c, h_ctx, mod_c, gn_g, skip, w_down, ln_g, ln_b, alpha=alpha, mod_row=0)
    return out_l, out_c


def _cv_in_kernel(h_ref, mod_ref, w_ref, b_ref, o_ref):
    d = h_ref.shape[-1]
    u = (h_ref[0] * (1.0 + mod_ref[0, 1:2, :]) + mod_ref[0, 0:1, :]).astype(BF16)
    y = jnp.dot(u, w_ref[...], preferred_element_type=F32) + b_ref[...]
    o_ref[0] = y[:, :d] * _sigmoid(y[:, d:])


def _cv_in(h, mod, w_in, b_in, *, mod_row=None):
    bsz, length, d = h.shape
    tm = min(length, 512)
    tok = lambda: pl.BlockSpec((1, tm, d), lambda b, i: (b, i, 0))
    return pl.pallas_call(
        _cv_in_kernel,
        grid=(bsz, length // tm),
        in_specs=[tok(), pl.BlockSpec((1, 8, d), _mod_map(mod_row)), _const_spec((d, 2 * d)),
                  _const_spec((1, 2 * d))],
        out_specs=tok(),
        out_shape=jax.ShapeDtypeStruct((bsz, length, d), F32),
        compiler_params=_cparams("parallel", "parallel"),
        name="cv_in",
    )(h, mod, w_in.astype(BF16), b_in.reshape(1, 2 * d))


def _cv_out_kernel(prev_ref, x_ref, next_ref, h_ref, mod_ref, dw_ref, db_ref, cg_ref, cb_ref, w_ref, b_ref,
                   lng_ref, lnb_ref, o_ref, sbuf, cbuf, *, alpha, taps):
    i = pl.program_id(1)
    nt = pl.num_programs(1)
    tm = x_ref.shape[1]
    hp = prev_ref.shape[1]
    rows = tm + 2 * hp
    sbuf[0, 0:hp, :] = jnp.where(i > 0, prev_ref[0], 0.0)
    sbuf[0, hp:hp + tm, :] = x_ref[0]
    sbuf[0, hp + tm:, :] = jnp.where(i < nt - 1, next_ref[0], 0.0)
    for s in range(1, 8):
        sbuf[s, 0:rows - 8, :] = sbuf[0, s:s + rows - 8, :]
    left = (taps - 1) // 2
    rb = 16

    def conv_rows(blk, carry):
        r0 = pl.multiple_of(blk * rb, rb)
        acc = None
        for k in range(taps):
            off = hp + k - left
            win = sbuf[off % 8, pl.ds(r0 + (off // 8) * 8, rb), :].reshape(rb // 8, 8, -1)
            term = dw_ref[k] * win
            acc = term if acc is None else acc + term
        cbuf[pl.ds(r0, rb), :] = acc.reshape(rb, -1)
        return carry

    lax.fori_loop(0, tm // rb, conv_rows, 0)
    c = _silu(_ln(cbuf[...] + db_ref[...], cg_ref[...], cb_ref[...])).astype(BF16)
    y = jnp.dot(c, w_ref[...], preferred_element_type=F32) + b_ref[...]
    g1 = 1.0 + mod_ref[0, 2:3, :]
    o_ref[0] = _ln(alpha * h_ref[0] + g1 * y, lng_ref[...], lnb_ref[...])


def _cv_out(hmid, h, mod, dw_w, dw_b, cln_g, cln_b, w_out, b_out, ln_g, ln_b, *, alpha, mod_row=None):
    bsz, length, d = h.shape
    taps = dw_w.shape[0]
    tm = min(length, 256)
    hp = 16
    assert (taps - 1) // 2 <= hp and taps // 2 <= hp
    r = tm // hp
    nb = length // hp
    dw = jnp.broadcast_to(dw_w[:, None, :], (taps, 8, d))
    tok = lambda: pl.BlockSpec((1, tm, d), lambda b, i: (b, i, 0))
    vec = lambda: _const_spec((1, d))
    return pl.pallas_call(
        functools.partial(_cv_out_kernel, alpha=alpha, taps=taps),
        grid=(bsz, length // tm),
        in_specs=[pl.BlockSpec((1, hp, d), lambda b, i: (b, jnp.maximum(i * r - 1, 0), 0)), tok(),
                  pl.BlockSpec((1, hp, d), lambda b, i: (b, jnp.minimum((i + 1) * r, nb - 1), 0)),
                  tok(), pl.BlockSpec((1, 8, d), _mod_map(mod_row)), _const_spec((taps, 8, d)), vec(), vec(), vec(),
                  _const_spec((d, d)), vec(), vec(), vec()],
        out_specs=tok(),
        out_shape=jax.ShapeDtypeStruct((bsz, length, d), F32),
        scratch_shapes=[pltpu.VMEM((8, tm + 2 * hp, d), F32), pltpu.VMEM((tm, d), F32)],
        compiler_params=_cparams("parallel", "parallel"),
        name="cv_out",
    )(hmid, hmid, hmid, h, mod, dw, dw_b.reshape(1, d), cln_g.reshape(1, d), cln_b.reshape(1, d),
      w_out.astype(BF16), b_out.reshape(1, d), ln_g.reshape(1, d), ln_b.reshape(1, d))


def kernel(x, c, ctx, c_ctx, mod_w, mod_b, post_ln_g, post_ln_b, ffn_w_gate, ffn_w_up, ffn_conv_w, ffn_conv_b, ffn_w_down, s5_lambda_re, s5_lambda_im, s5_log_dt, s5_b_re, s5_b_im, s5_c_re, s5_c_im, s5_d, s5_w_glu, s5_b_glu, ml_w_up, ml_conv_w, ml_conv_b, ml_w_q, ml_w_k, ml_w_v, ml_w_gates, ml_b_gates, ml_gn_g, ml_skip, ml_w_down, cv_w_in, cv_b_in, cv_dw_w, cv_dw_b, cv_ln_g, cv_ln_b, cv_w_out, cv_b_out):
    bsz, length, d = x.shape
    depth = mod_w.shape[0]
    ctx_len = ctx.shape[1]
    rows = length // GRID_W
    alpha = (2 * depth) ** 0.25

    modv = _modulation(c, c_ctx, mod_w, mod_b)
    pad2 = jnp.zeros((2, d), F32)
    h_lat, h_ctx = x, ctx
    for i in range(depth):
        kind, occ = i % N_MIXERS, i // N_MIXERS
        last = i == depth - 1
        mod_l = jnp.concatenate([modv[i, :bsz].reshape(bsz, 6, d), jnp.zeros((bsz, 2, d), F32)], axis=1)
        mod_c = jnp.concatenate([modv[i, bsz].reshape(6, d), pad2], axis=0)[None]
        col_major = (kind != 2) and (occ % 2 == 1)
        ln1 = (post_ln_g[i, 0], post_ln_b[i, 0])
        ffn = (ffn_w_gate[i], ffn_w_up[i], ffn_conv_w[i], ffn_conv_b[i], ffn_w_down[i],
               post_ln_g[i, 1], post_ln_b[i, 1])
        if kind == 0:
            s5 = (s5_lambda_re[occ], s5_lambda_im[occ], s5_log_dt[occ], s5_b_re[occ], s5_b_im[occ],
                  s5_c_re[occ], s5_c_im[occ], s5_d[occ])
            y_lat, y_ctx = _s5_core(h_lat, h_ctx, mod_l, mod_c, s5, col_major)
            h_lat = _glu_ln(y_lat, h_lat, mod_l, s5_w_glu[occ], s5_b_glu[occ], *ln1, alpha=alpha)
            if not last:
                h_ctx = _glu_ln(y_ctx, h_ctx, mod_c, s5_w_glu[occ], s5_b_glu[occ], *ln1, alpha=alpha, mod_row=0)
        elif kind == 1:
            ml = (ml_w_up[occ], ml_conv_w[occ], ml_conv_b[occ], ml_w_q[occ], ml_w_k[occ], ml_w_v[occ],
                  ml_w_gates[occ], ml_b_gates[occ], ml_gn_g[occ], ml_skip[occ], ml_w_down[occ])
            h_in = _grid_transpose(h_lat, rows, GRID_W) if col_major else h_lat
            h_lat, h_ctx = _mlstm_layer(h_in, h_ctx, mod_l, mod_c, ml, *ln1, alpha=alpha, need_ctx=not last)
            if col_major:
                h_lat = _grid_transpose(h_lat, GRID_W, rows)
        else:
            hm_l = _cv_in(h_lat, mod_l, cv_w_in[occ], cv_b_in[occ])
            cv = (cv_dw_w[occ], cv_dw_b[occ], cv_ln_g[occ], cv_ln_b[occ], cv_w_out[occ], cv_b_out[occ])
            if not last:
                hm_c = _cv_in(h_ctx, mod_c, cv_w_in[occ], cv_b_in[occ], mod_row=0)
                h_ctx = _cv_out(hm_c, h_ctx, mod_c, *cv, *ln1, alpha=alpha, mod_row=0)
            h_lat = _cv_out(hm_l, h_lat, mod_l, *cv, *ln1, alpha=alpha)
        h_lat = _conv_ffn_ln(h_lat, mod_l, *ffn, width=GRID_W, alpha=alpha)
        if not last:
            h_ctx = _conv_ffn_ln(h_ctx, mod_c, *ffn, width=ctx_len, alpha=alpha, mod_row=0)
    return h_lat
```
